```python
import jax, jax.numpy as jnp
from jax import lax
import numpy as np


D_MODEL = 1024
BATCH = 2
SEQ = 8192
DEPTH = 2

GRID_W = 64
CTX_LEN = 256
N_MOD = 9
NA_HEADS = 8
HEAD_DIM = 64
NA_WIDTH = NA_HEADS * HEAD_DIM
NA_KH = 8
NA_KW = 16
POOL_GROUPS = 4
POOL_CH = 128
POOL_WIDTH = POOL_GROUPS * POOL_CH
POOL_WINDOWS = (2, 4, 8, 16)
MIX_WIDTH = NA_WIDTH + POOL_WIDTH
IN_WIDTH = 3 * NA_WIDTH + POOL_WIDTH
D_FF = 2816
ROPE_THETA = 10000.0
ROPE_PAIRS = HEAD_DIM // 4
RMS_EPS = 1e-6
NEG_INF = -1e30

kernel_name = 'hybrid_na_pool_macaron_dit_block'


def rms_norm(x, g):
    xf = x.astype(jnp.float32)
    y = xf * lax.rsqrt(jnp.mean(xf * xf, axis=-1, keepdims=True) + RMS_EPS)
    return (y * g.astype(jnp.float32)).astype(x.dtype)


def modulate(h, shift, scale):
    return h * (1.0 + scale) + shift


def mod_vectors(cvec, w_mod, b_mod):
    m = jax.nn.silu(cvec) @ w_mod + b_mod
    return jnp.split(m, N_MOD, axis=-1)


def swiglu(h, w_gate_up, w_down):
    gate, up = jnp.split(h @ w_gate_up, 2, axis=-1)
    return (jax.nn.silu(gate) * up) @ w_down


def sandwich_ffn(x, w_gate_up, w_down, g_pre, g_post, shift, scale, gate):
    h = modulate(rms_norm(x, g_pre), shift, scale)
    return x + 0.5 * gate * rms_norm(swiglu(h, w_gate_up, w_down), g_post)


def axial_rope(x, rows):
    seq = rows * GRID_W
    t = jnp.arange(seq)
    inv = ROPE_THETA ** (-jnp.arange(ROPE_PAIRS, dtype=jnp.float32) / ROPE_PAIRS)

    def rot(xa, pos):
        ang = pos.astype(jnp.float32)[:, None] * inv
        cos = jnp.cos(ang)[:, None, :]
        sin = jnp.sin(ang)[:, None, :]
        x1, x2 = jnp.split(xa, 2, axis=-1)
        return jnp.concatenate([x1 * cos - x2 * sin, x2 * cos + x1 * sin], axis=-1)

    xr, xc = jnp.split(x.astype(jnp.float32), 2, axis=-1)
    return jnp.concatenate([rot(xr, t // GRID_W), rot(xc, t % GRID_W)], axis=-1).astype(x.dtype)


def neighbourhood_attention(q, k, v, kc, vc, rpb, rows):
    b, s, h, dh = q.shape
    kh = min(NA_KH, rows)
    scale = dh ** -0.5
    q = q.reshape(b, rows, GRID_W, h, dh)
    k = k.reshape(b, rows, GRID_W, h, dh)
    v = v.reshape(b, rows, GRID_W, h, dh)
    r = np.arange(rows)
    row_start = np.clip(r - kh // 2, 0, rows - kh)
    row_idx = row_start[:, None] + np.arange(kh)[None, :]
    k_blk = k[:, row_idx]
    v_blk = v[:, row_idx]
    j = np.arange(GRID_W)
    col_start = np.clip(j - NA_KW // 2, 0, GRID_W - NA_KW)
    col_valid = (j[None, :] >= col_start[:, None]) & (j[None, :] < col_start[:, None] + NA_KW)
    dr = row_idx - r[:, None]
    dc = np.clip(j[None, :] - j[:, None] + NA_KW - 1, 0, 2 * NA_KW - 2)
    bias = rpb[:, dr + NA_KH - 1]
    bias = bias[..., dc]
    bias = jnp.transpose(bias, (0, 1, 3, 2, 4)).astype(jnp.float32)
    s_loc = jnp.einsum('brqhd,brkchd->bhrqkc', q, k_blk, preferred_element_type=jnp.float32) * scale
    s_loc = jnp.where(col_valid[:, None, :], s_loc + bias, NEG_INF)
    s_ctx = jnp.einsum('brqhd,bkhd->bhrqk', q, kc, preferred_element_type=jnp.float32) * scale
    n_loc = kh * GRID_W
    scores = jnp.concatenate([s_loc.reshape(b, h, rows, GRID_W, n_loc), s_ctx], axis=-1)
    p = jax.nn.softmax(scores, axis=-1).astype(v.dtype)
    p_loc = p[..., :n_loc].reshape(b, h, rows, GRID_W, kh, GRID_W)
    p_ctx = p[..., n_loc:]
    o = jnp.einsum('bhrqkc,brkchd->brqhd', p_loc, v_blk) + jnp.einsum('bhrqk,bkhd->brqhd', p_ctx, vc)
    return o.reshape(b, s, h * dh)


def context_attention(q, k, v):
    b, l, h, dh = q.shape
    sc = jnp.einsum('bqhd,bkhd->bhqk', q, k, preferred_element_type=jnp.float32) * dh ** -0.5
    p = jax.nn.softmax(sc, axis=-1).astype(v.dtype)
    return jnp.einsum('bhqk,bkhd->bqhd', p, v).reshape(b, l, h * dh)


def pool_mix(u, w_pool, pool_scale):
    length = u.shape[-2]
    ug = u.reshape(u.shape[:-1] + (POOL_GROUPS, POOL_CH))
    uf = ug.astype(jnp.float32)
    cs = jnp.cumsum(uf, axis=-3)
    pad = [(0, 0)] * cs.ndim
    pad[-3] = (1, 0)
    cs = jnp.pad(cs, pad)
    t = np.arange(length)
    outs = []
    for g, w in enumerate(POOL_WINDOWS):
        lo = np.clip(t - w // 2, 0, length)
        hi = np.clip(t - w // 2 + w, 0, length)
        cnt = (hi - lo).astype(np.float32)[:, None]
        csg = cs[..., g, :]
        outs.append((jnp.take(csg, hi, axis=-2) - jnp.take(csg, lo, axis=-2)) / cnt)
    pooled = jnp.stack(outs, axis=-2)
    d = (pooled - uf).astype(u.dtype)
    y = jnp.einsum('...gc,gcd->...gd', d, w_pool) * pool_scale.reshape(POOL_GROUPS, POOL_CH)
    return y.reshape(u.shape)


def token_mix(hx, hc, w_in, w_out, rpb, w_pool, pool_scale, with_ctx_out):
    b, s, _ = hx.shape
    rows = s // GRID_W
    l = hc.shape[1]
    qx, kx, vx, ux = jnp.split(hx @ w_in, [NA_WIDTH, 2 * NA_WIDTH, 3 * NA_WIDTH], axis=-1)
    if with_ctx_out:
        qc, kc, vc, uc = jnp.split(hc @ w_in, [NA_WIDTH, 2 * NA_WIDTH, 3 * NA_WIDTH], axis=-1)
    else:
        kc, vc = jnp.split(hc @ w_in[:, NA_WIDTH:3 * NA_WIDTH], 2, axis=-1)
    heads = lambda t, n: t.reshape(b, n, NA_HEADS, HEAD_DIM)
    qx = axial_rope(heads(qx, s), rows)
    kx = axial_rope(heads(kx, s), rows)
    kc_h, vc_h = heads(kc, l), heads(vc, l)
    na_x = neighbourhood_attention(qx, kx, heads(vx, s), kc_h, vc_h, rpb, rows)
    pool_x = pool_mix(ux.reshape(b, rows, GRID_W, POOL_WIDTH), w_pool, pool_scale).reshape(b, s, POOL_WIDTH)
    out_x = jnp.concatenate([na_x, pool_x], axis=-1) @ w_out
    if with_ctx_out:
        na_c = context_attention(heads(qc, l), kc_h, vc_h)
        pool_c = pool_mix(uc, w_pool, pool_scale)
        out_c = jnp.concatenate([na_c, pool_c], axis=-1) @ w_out
        return out_x, out_c
    return out_x, None


def setup_inputs(seed: int = 0) -> dict:
    key = jax.random.key(seed)
    ks = jax.random.split(key, 14)
    nrm = jax.random.normal
    f32 = jnp.float32
    return {
        'x': nrm(ks[0], (BATCH, SEQ, D_MODEL), f32),
        'c': nrm(ks[1], (BATCH, D_MODEL), f32),
        'ctx': nrm(ks[2], (BATCH, CTX_LEN, D_MODEL), f32),
        'c_ctx': nrm(ks[3], (D_MODEL,), f32),
        'w_mod': nrm(ks[4], (DEPTH, D_MODEL, N_MOD * D_MODEL), f32) * (0.5 * D_MODEL ** -0.5),
        'b_mod': nrm(ks[5], (DEPTH, N_MOD * D_MODEL), f32) * 0.01,
        'norm_g': 1.0 + 0.05 * nrm(ks[6], (DEPTH, 6, D_MODEL), f32),
        'w_ffn_gate_up': nrm(ks[7], (DEPTH, 2, D_MODEL, 2 * D_FF), f32) * D_MODEL ** -0.5,
        'w_ffn_down': nrm(ks[8], (DEPTH, 2, D_FF, D_MODEL), f32) * D_FF ** -0.5,
        'w_in': nrm(ks[9], (DEPTH, D_MODEL, IN_WIDTH), f32) * D_MODEL ** -0.5,
        'w_out': nrm(ks[10], (DEPTH, MIX_WIDTH, D_MODEL), f32) * MIX_WIDTH ** -0.5,
        'na_rpb': nrm(ks[11], (DEPTH, NA_HEADS, 2 * NA_KH - 1, 2 * NA_KW - 1), f32) * 0.1,
        'w_pool': nrm(ks[12], (DEPTH, POOL_GROUPS, POOL_CH, POOL_CH), f32) * POOL_CH ** -0.5,
        'pool_scale': 1.0 + 0.05 * nrm(ks[13], (DEPTH, POOL_WIDTH), f32),
    }


def reference(x, c, ctx, c_ctx, w_mod, b_mod, norm_g, w_ffn_gate_up, w_ffn_down, w_in, w_out, na_rpb, w_pool, pool_scale):
    for l in range(DEPTH):
        last = l == DEPTH - 1
        mx = [m[:, None, :] for m in mod_vectors(c, w_mod[l], b_mod[l])]
        mc = mod_vectors(c_ctx, w_mod[l], b_mod[l])
        g = norm_g[l]
        x = sandwich_ffn(x, w_ffn_gate_up[l, 0], w_ffn_down[l, 0], g[0], g[1], mx[0], mx[1], mx[2])
        ctx = sandwich_ffn(ctx, w_ffn_gate_up[l, 0], w_ffn_down[l, 0], g[0], g[1], mc[0], mc[1], mc[2])
        hx = modulate(rms_norm(x, g[2]), mx[3], mx[4])
        hc = modulate(rms_norm(ctx, g[2]), mc[3], mc[4])
        out_x, out_c = token_mix(hx, hc, w_in[l], w_out[l], na_rpb[l], w_pool[l], pool_scale[l], not last)
        x = x + mx[5] * rms_norm(out_x, g[3])
        x = sandwich_ffn(x, w_ffn_gate_up[l, 1], w_ffn_down[l, 1], g[4], g[5], mx[6], mx[7], mx[8])
        if not last:
            ctx = ctx + mc[5] * rms_norm(out_c, g[3])
            ctx = sandwich_ffn(ctx, w_ffn_gate_up[l, 1], w_ffn_down[l, 1], g[4], g[5], mc[6], mc[7], mc[8])
    return x
```

```python
import functools

import numpy as np
import jax
import jax.numpy as jnp
from jax import lax
from jax.experimental import pallas as pl
from jax.experimental.pallas import tpu as pltpu

GRID_W = 64
N_MOD = 9
NA_HEADS = 8
HEAD_DIM = 64
NA_WIDTH = NA_HEADS * HEAD_DIM
NA_KH = 8
NA_KW = 16
POOL_GROUPS = 4
POOL_CH = 128
POOL_WIDTH = POOL_GROUPS * POOL_CH
POOL_WINDOWS = (2, 4, 8, 16)
ROPE_THETA = 10000.0
ROPE_PAIRS = HEAD_DIM // 4
RMS_EPS = 1e-6
NEG_INF = -1e30

V7X_LANES = 128
V7X_SUBLANES = 8
V7X_VMEM_BYTES = 64 * 1024 * 1024

FF_CHUNK = 256
ATT_ROWS = 4
ATT_TQ = ATT_ROWS * GRID_W

F32 = jnp.float32
BF16 = jnp.bfloat16


def _vmem_limit(nbytes):
    return int(min(nbytes + (8 << 20), V7X_VMEM_BYTES - (6 << 20)))


def _resident(block_shape):
    zeros = (0,) * len(block_shape)
    return pl.BlockSpec(block_shape, lambda *_: zeros, pipeline_mode=pl.Buffered(1))


def _rms(x, g):
    return x * lax.rsqrt(jnp.mean(x * x, axis=-1, keepdims=True) + RMS_EPS) * g


def _dot(a, b):
    return jnp.dot(a, b, preferred_element_type=F32)


def _dot_nt(a, b):
    return lax.dot_general(a, b, (((1,), (1,)), ((), ())), preferred_element_type=F32)


def _split_bf16(x):
    hi = x.astype(BF16)
    lo = (x - hi.astype(F32)).astype(BF16)
    return hi, lo


def _mod_kernel(cv_ref, w_ref, b_ref, o_ref):
    s = cv_ref[...]
    s = s * jax.nn.sigmoid(s)
    s_hi, s_lo = _split_bf16(s)
    w_hi, w_lo = _split_bf16(w_ref[...])
    n = s.shape[0]
    r = _dot(jnp.concatenate([s_hi, s_lo], axis=0), w_hi)
    o_ref[...] = r[:n] + r[n:] + _dot(s_hi, w_lo) + b_ref[...]


def _mod_vectors(cv, w_mod, b_mod):
    depth, d, n = w_mod.shape
    tn = 1024
    return pl.pallas_call(
        _mod_kernel,
        out_shape=jax.ShapeDtypeStruct((depth, cv.shape[0], n), F32),
        grid=(depth, n // tn),
        in_specs=[
            pl.BlockSpec(cv.shape, lambda l, j: (0, 0)),
            pl.BlockSpec((None, d, tn), lambda l, j: (l, 0, j)),
            pl.BlockSpec((None, 1, tn), lambda l, j: (l, 0, j)),
        ],
        out_specs=pl.BlockSpec((None, cv.shape[0], tn), lambda l, j: (l, 0, j)),
        compiler_params=pltpu.CompilerParams(
            dimension_semantics=("arbitrary", "arbitrary"),
            vmem_limit_bytes=_vmem_limit(5 * d * tn * 4)),
        name="mod_vectors",
    )(cv, w_mod, b_mod.reshape(depth, 1, n))


def _ffn_kernel(x_ref, mod_ref, g_ref, wg_ref, wu_ref, wd_ref, o_ref, h_scr, acc_scr):
    x = x_ref[...]
    h = _rms(x, g_ref[0]) * (1.0 + mod_ref[1]) + mod_ref[0]
    h_scr[...] = h.astype(BF16)
    acc_scr[...] = jnp.zeros_like(acc_scr)

    def chunk(j, carry):
        hb = h_scr[...]
        gate = _dot(hb, wg_ref[j])
        up = _dot(hb, wu_ref[j])
        act = (gate * jax.nn.sigmoid(gate) * up).astype(BF16)
        acc_scr[...] += _dot(act, wd_ref[j])
        return carry

    lax.fori_loop(0, wg_ref.shape[0], chunk, 0)
    y = _rms(acc_scr[...], g_ref[1])
    o_ref[...] = x_ref[...] + 0.5 * mod_ref[2] * y


def _ffn(xt, mod, g2, wg, wu, wd, tm):
    n, d = xt.shape
    nb = n // tm
    per_row = nb // mod.shape[0]
    nc, _, tf = wg.shape
    vmem = 4 * tm * d * 4 + tm * d * 2 + tm * d * 4 + 3 * nc * d * tf * 2 + 4 * tm * tf * 4
    return pl.pallas_call(
        _ffn_kernel,
        out_shape=jax.ShapeDtypeStruct((n, d), F32),
        grid=(nb,),
        in_specs=[
            pl.BlockSpec((tm, d), lambda i: (i, 0)),
            pl.BlockSpec((None, 3, 1, d), lambda i: (i // per_row, 0, 0, 0)),
            pl.BlockSpec((2, 1, d), lambda i: (0, 0, 0)),
            _resident(wg.shape), _resident(wu.shape), _resident(wd.shape),
        ],
        out_specs=pl.BlockSpec((tm, d), lambda i: (i, 0)),
        scratch_shapes=[pltpu.VMEM((tm, d), BF16), pltpu.VMEM((tm, d), F32)],
        compiler_params=pltpu.CompilerParams(
            dimension_semantics=("arbitrary",), vmem_limit_bytes=_vmem_limit(vmem)),
        name="sandwich_ffn",
    )(xt, mod, g2, wg, wu, wd)


def _inproj_kernel(x_ref, mod_ref, g_ref, w_ref, cos_ref, sa_ref, sb_ref, band_ref, icnt_ref,
                   wp_ref, ps_ref, q_ref, k_ref, v_ref, p_ref):
    h = (_rms(x_ref[...], g_ref[0]) * (1.0 + mod_ref[1]) + mod_ref[0]).astype(BF16)
    cos, sa, sb = cos_ref[...], sa_ref[...], sb_ref[...]

    def rope(t):
        up = pltpu.roll(t, V7X_LANES - ROPE_PAIRS, 1)
        down = pltpu.roll(t, ROPE_PAIRS, 1)
        return t * cos + up * sa + down * sb

    scale = HEAD_DIM ** -0.5
    for c in range(NA_WIDTH // V7X_LANES):
        lanes = pl.ds(c * V7X_LANES, V7X_LANES)
        q_ref[:, lanes] = (rope(_dot(h, w_ref[:, pl.ds(c * V7X_LANES, V7X_LANES)])) * scale).astype(BF16)
        k_ref[:, lanes] = rope(_dot(h, w_ref[:, pl.ds(NA_WIDTH + c * V7X_LANES, V7X_LANES)])).astype(BF16)
    v_ref[...] = _dot(h, w_ref[:, pl.ds(2 * NA_WIDTH, NA_WIDTH)]).astype(BF16)
    for g in range(POOL_GROUPS):
        lanes = pl.ds(g * POOL_CH, POOL_CH)
        u = _dot(h, w_ref[:, pl.ds(3 * NA_WIDTH + g * POOL_CH, POOL_CH)])
        u_hi, u_lo = _split_bf16(u)
        band = band_ref[g]
        win = _dot(band, u_hi) + _dot(band, u_lo)
        dlt = (win * icnt_ref[:, lanes] - u).astype(BF16)
        p_ref[:, lanes] = (_dot(dlt, wp_ref[g]) * ps_ref[:, lanes]).astype(BF16)


def _inproj(xt, mod, g1, w_in, rope_tabs, band, icnt, w_pool, pool_scale, tm):
    n, d = xt.shape
    nb = n // tm
    per_row = nb // mod.shape[0]
    tab_blocks = rope_tabs[0].shape[0] // tm
    tab_spec = pl.BlockSpec((tm, V7X_LANES), lambda i: (i % tab_blocks, 0))
    out = jax.ShapeDtypeStruct((n, NA_WIDTH), BF16)
    out_spec = pl.BlockSpec((tm, NA_WIDTH), lambda i: (i, 0))
    vmem = (2 * tm * d * 4 + tm * d * 2 + w_in.size * 2 + 6 * tm * V7X_LANES * 4 + band.size * 2
            + icnt.size * 4 + 8 * tm * NA_WIDTH * 2 + 6 * tm * NA_WIDTH * 4)
    return pl.pallas_call(
        _inproj_kernel,
        out_shape=(out, out, out, out),
        grid=(nb,),
        in_specs=[
            pl.BlockSpec((tm, d), lambda i: (i, 0)),
            pl.BlockSpec((None, 2, 1, d), lambda i: (i // per_row, 0, 0, 0)),
            pl.BlockSpec((1, 1, d), lambda i: (0, 0, 0)),
            _resident(w_in.shape),
            tab_spec, tab_spec, tab_spec,
            _resident(band.shape), _resident(icnt.shape), _resident(w_pool.shape),
            pl.BlockSpec((1, POOL_WIDTH), lambda i: (0, 0)),
        ],
        out_specs=(out_spec, out_spec, out_spec, out_spec),
        compiler_params=pltpu.CompilerParams(
            dimension_semantics=("arbitrary",), vmem_limit_bytes=_vmem_limit(vmem)),
        name="in_projection",
    )(xt, mod, g1, w_in, *rope_tabs, band, icnt, w_pool, pool_scale)


def _softmax_pv(scores, values):
    m = functools.reduce(jnp.maximum, [jnp.max(s, axis=-1, keepdims=True) for s in scores])
    probs = [jnp.exp(s - m) for s in scores]
    denom = functools.reduce(lambda a, b: a + b, [jnp.sum(p, axis=-1, keepdims=True) for p in probs])
    o = functools.reduce(lambda a, b: a + b, [_dot(p.astype(BF16), v) for p, v in zip(probs, values)])
    return o / denom


def _na_kernel(q_ref, kp_ref, kc_ref, kn_ref, vp_ref, vc_ref, vn_ref, kx_ref, vx_ref, bias_ref, o_ref):
    for h in range(NA_HEADS):
        cols = pl.ds(h * HEAD_DIM, HEAD_DIM)
        q = q_ref[:, cols]
        scores = [
            _dot_nt(q, k_ref[:, cols]) + bias_ref[h, :, pl.ds(t * ATT_TQ, ATT_TQ)]
            for t, k_ref in enumerate((kp_ref, kc_ref, kn_ref))
        ]
        scores.append(_dot_nt(q, kx_ref[:, cols]))
        values = [v_ref[:, cols] for v_ref in (vp_ref, vc_ref, vn_ref, vx_ref)]
        o_ref[:, cols] = _softmax_pv(scores, values).astype(o_ref.dtype)


def _neighbourhood_attention(q, k, v, kx, vx, bias, batch, ctx_len):
    n = q.shape[0]
    nblk = n // batch // ATT_TQ
    cur = lambda b, i: (b * nblk + i, 0)
    prev = lambda b, i: (b * nblk + jnp.maximum(i - 1, 0), 0)
    nxt = lambda b, i: (b * nblk + jnp.minimum(i + 1, nblk - 1), 0)
    variant = lambda b, i: (jnp.where(i == 0, 0, jnp.where(i == nblk - 1, 2, 1)), 0, 0, 0)
    blk = lambda f: pl.BlockSpec((ATT_TQ, NA_WIDTH), f)
    ctx = pl.BlockSpec((ctx_len, NA_WIDTH), lambda b, i: (b, 0))
    vmem = 2 * bias[0].size * 4 + 16 * ATT_TQ * NA_WIDTH * 2 + 4 * ctx_len * NA_WIDTH * 2 + 16 * ATT_TQ * ATT_TQ * 4
    return pl.pallas_call(
        _na_kernel,
        out_shape=jax.ShapeDtypeStruct((n, NA_WIDTH), BF16),
        grid=(batch, nblk),
        in_specs=[blk(cur), blk(prev), blk(cur), blk(nxt), blk(prev), blk(cur), blk(nxt), ctx, ctx,
                  pl.BlockSpec((None,) + bias.shape[1:], variant)],
        out_specs=blk(cur),
        compiler_params=pltpu.CompilerParams(
            dimension_semantics=("arbitrary", "arbitrary"), vmem_limit_bytes=_vmem_limit(vmem)),
        name="neighbourhood_attention",
    )(q, k, k, k, v, v, v, kx, vx, bias)


def _ctx_attn_kernel(q_ref, k_ref, v_ref, o_ref):
    for h in range(NA_HEADS):
        cols = pl.ds(h * HEAD_DIM, HEAD_DIM)
        s = _dot_nt(q_ref[:, cols], k_ref[:, cols])
        o_ref[:, cols] = _softmax_pv([s], [v_ref[:, cols]]).astype(o_ref.dtype)


def _context_attention(q, k, v, batch):
    n = q.shape[0]
    spec = pl.BlockSpec((n // batch, NA_WIDTH), lambda b: (b, 0))
    return pl.pallas_call(
        _ctx_attn_kernel,
        out_shape=jax.ShapeDtypeStruct((n, NA_WIDTH), BF16),
        grid=(batch,),
        in_specs=[spec, spec, spec],
        out_specs=spec,
        compiler_params=pltpu.CompilerParams(dimension_semantics=("arbitrary",)),
        name="context_attention",
    )(q, k, v)


def _outproj_kernel(x_ref, a_ref, p_ref, w_ref, g_ref, gate_ref, o_ref):
    y = _dot(a_ref[...], w_ref[pl.ds(0, NA_WIDTH), :]) + _dot(p_ref[...], w_ref[pl.ds(NA_WIDTH, POOL_WIDTH), :])
    o_ref[...] = x_ref[...] + gate_ref[0] * _rms(y, g_ref[0])


def _outproj(xt, att, pool, w_out, g1, gate, tm):
    n, d = xt.shape
    nb = n // tm
    per_row = nb // gate.shape[0]
    half = pl.BlockSpec((tm, NA_WIDTH), lambda i: (i, 0))
    vmem = 4 * tm * d * 4 + 4 * tm * NA_WIDTH * 2 + w_out.size * 2 + 2 * tm * d * 4
    return pl.pallas_call(
        _outproj_kernel,
        out_shape=jax.ShapeDtypeStruct((n, d), F32),
        grid=(nb,),
        in_specs=[
            pl.BlockSpec((tm, d), lambda i: (i, 0)), half, half,
            _resident(w_out.shape),
            pl.BlockSpec((1, 1, d), lambda i: (0, 0, 0)),
            pl.BlockSpec((None, 1, 1, d), lambda i: (i // per_row, 0, 0, 0)),
        ],
        out_specs=pl.BlockSpec((tm, d), lambda i: (i, 0)),
        compiler_params=pltpu.CompilerParams(
            dimension_semantics=("arbitrary",), vmem_limit_bytes=_vmem_limit(vmem)),
        name="out_projection",
    )(xt, att, pool, w_out, g1, gate)


def _rope_tables(seq):
    t = np.arange(seq)
    inv = ROPE_THETA ** (-np.arange(ROPE_PAIRS, dtype=np.float64) / ROPE_PAIRS)
    ang_r = (t // GRID_W)[:, None] * inv
    ang_c = (t % GRID_W)[:, None] * inv
    ang = np.concatenate([ang_r, ang_r, ang_c, ang_c], axis=1)
    first = np.tile(np.arange(HEAD_DIM) % (2 * ROPE_PAIRS) < ROPE_PAIRS, 2)
    cos = np.tile(np.cos(ang), (1, 2))
    sin = np.tile(np.sin(ang), (1, 2))
    sa = np.where(first, -sin, 0.0)
    sb = np.where(first, 0.0, sin)
    return tuple(jnp.asarray(a, F32) for a in (cos, sa, sb))


def _identity_rope_tables(n):
    ones = jnp.ones((n, V7X_LANES), F32)
    zeros = jnp.zeros((n, V7X_LANES), F32)
    return ones, zeros, zeros


def _pool_tables(tm, length):
    t = np.arange(tm)
    pos = t % length
    base = t - pos
    bands, icnts = [], []
    for w in POOL_WINDOWS:
        lo = np.clip(pos - w // 2, 0, length)
        hi = np.clip(pos - w // 2 + w, 0, length)
        s = t[None, :]
        bands.append((s >= (base + lo)[:, None]) & (s < (base + hi)[:, None]))
        icnts.append(np.repeat((1.0 / (hi - lo))[:, None], POOL_CH, axis=1))
    band = jnp.asarray(np.stack(bands), BF16)
    icnt = jnp.asarray(np.concatenate(icnts, axis=1), F32)
    return band, icnt


def _attention_bias(rpb, rows):
    i = np.arange(ATT_TQ) // GRID_W
    j = np.arange(ATT_TQ) % GRID_W
    m = np.arange(3 * ATT_TQ) // GRID_W
    c = np.arange(3 * ATT_TQ) % GRID_W
    col_start = np.clip(j - NA_KW // 2, 0, GRID_W - NA_KW)
    col_ok = (c[None, :] >= col_start[:, None]) & (c[None, :] < col_start[:, None] + NA_KW)
    dc = np.clip(c[None, :] - j[:, None] + NA_KW - 1, 0, 2 * NA_KW - 2)
    kh = min(NA_KH, rows)
    out = []
    for r0 in (0, ATT_ROWS, rows - ATT_ROWS):
        r = r0 + i
        kr = r0 - ATT_ROWS + m
        row_start = np.clip(r - kh // 2, 0, rows - kh)
        row_ok = (kr[None, :] >= row_start[:, None]) & (kr[None, :] < row_start[:, None] + kh)
        dr = np.clip(kr[None, :] - r[:, None] + NA_KH - 1, 0, 2 * NA_KH - 2)
        out.append(jnp.where((row_ok & col_ok)[None], rpb[:, dr, dc], NEG_INF))
    return jnp.stack(out).astype(F32)


def kernel(x, c, ctx, c_ctx, w_mod, b_mod, norm_g, w_ffn_gate_up, w_ffn_down, w_in, w_out, na_rpb, w_pool, pool_scale):
    batch, seq, d = x.shape
    ctx_len = ctx.shape[1]
    depth = w_mod.shape[0]
    d_ff = w_ffn_down.shape[2]
    rows = seq // GRID_W
    n_chunks = d_ff // FF_CHUNK
    tm_x, tm_c = 512, ctx_len

    cv = jnp.zeros((V7X_SUBLANES, d), F32).at[:batch].set(c).at[batch].set(c_ctx)
    mods = _mod_vectors(cv, w_mod, b_mod)[:, :batch + 1].reshape(depth, batch + 1, N_MOD, 1, d)

    wgu = w_ffn_gate_up.astype(BF16).reshape(depth, 2, d, 2, n_chunks, FF_CHUNK)
    wgu = jnp.transpose(wgu, (0, 1, 3, 4, 2, 5))
    wdn = w_ffn_down.astype(BF16).reshape(depth, 2, n_chunks, FF_CHUNK, d)
    w_in_b, w_out_b, w_pool_b = w_in.astype(BF16), w_out.astype(BF16), w_pool.astype(BF16)
    norm = norm_g.reshape(depth, 6, 1, d)

    rope_x = _rope_tables(seq)
    rope_c = _identity_rope_tables(tm_c)
    band_x, icnt_x = _pool_tables(tm_x, GRID_W)
    band_c, icnt_c = _pool_tables(tm_c, ctx_len)

    xt = x.reshape(batch * seq, d)
    ct = ctx.reshape(batch * ctx_len, d)
    for l in range(depth):
        last = l == depth - 1
        mx, mc = mods[l, :batch], mods[l, batch:]
        g = norm[l]
        ps = pool_scale[l].reshape(1, POOL_WIDTH)
        ffn = lambda t, m, f, tm: _ffn(t, m, g[4 * f:4 * f + 2], wgu[l, f, 0], wgu[l, f, 1], wdn[l, f], tm)
        xt = ffn(xt, mx[:, 0:3], 0, tm_x)
        ct = ffn(ct, mc[:, 0:3], 0, tm_c)
        qx, kx, vx, px = _inproj(xt, mx[:, 3:5], g[2:3], w_in_b[l], rope_x, band_x, icnt_x, w_pool_b[l], ps, tm_x)
        qc, kc, vc, pc = _inproj(ct, mc[:, 3:5], g[2:3], w_in_b[l], rope_c, band_c, icnt_c, w_pool_b[l], ps, tm_c)
        bias = _attention_bias(na_rpb[l], rows)
        ax = _neighbourhood_attention(qx, kx, vx, kc, vc, bias, batch, ctx_len)
        xt = _outproj(xt, ax, px, w_out_b[l], g[3:4], mx[:, 5:6], tm_x)
        xt = ffn(xt, mx[:, 6:9], 1, tm_x)
        if not last:
            ac = _context_attention(qc, kc, vc, batch)
            ct = _outproj(ct, ac, pc, w_out_b[l], g[3:4], mc[:, 5:6], tm_c)
            ct = ffn(ct, mc[:, 6:9], 1, tm_c)
    return xt.reshape(batch, seq, d)
```

```python
import functools

import numpy as np
import jax
import jax.numpy as jnp
from jax import lax
from jax.experimental import pallas as pl
from jax.experimental.pallas import tpu as pltpu

GRID_W = 64
N_MOD = 9
NA_HEADS = 8
HEAD_DIM = 64
NA_WIDTH = NA_HEADS * HEAD_DIM
NA_KH = 8
NA_KW = 16
POOL_GROUPS = 4
POOL_CH = 128
POOL_WIDTH = POOL_GROUPS * POOL_CH
POOL_WINDOWS = (2, 4, 8, 16)
ROPE_THETA = 10000.0
ROPE_PAIRS = HEAD_DIM // 4
RMS_EPS = 1e-6
NEG_INF = -1e30

V7X_LANES = 128
V7X_SUBLANES = 8
V7X_VMEM_BYTES = 64 * 1024 * 1024

FF_CHUNK = 256
ATT_ROWS = 4
ATT_TQ = ATT_ROWS * GRID_W

F32 = jnp.float32
BF16 = jnp.bfloat16


def _vmem_limit(nbytes):
    return int(min(nbytes + (8 << 20), V7X_VMEM_BYTES - (6 << 20)))


def _resident(block_shape):
    zeros = (0,) * len(block_shape)
    return pl.BlockSpec(block_shape, lambda *_: zeros, pipeline_mode=pl.Buffered(1))


def _rms(x, g):
    return x * lax.rsqrt(jnp.mean(x * x, axis=-1, keepdims=True) + RMS_EPS) * g


def _dot(a, b):
    return jnp.dot(a, b, preferred_element_type=F32)


def _dot_nt(a, b):
    return lax.dot_general(a, b, (((1,), (1,)), ((), ())), preferred_element_type=F32)


def _split_bf16(x):
    hi = x.astype(BF16)
    lo = (x - hi.astype(F32)).astype(BF16)
    return hi, lo


def _mod_kernel(cv_ref, w_ref, b_ref, o_ref):
    s = cv_ref[...]
    s = s * jax.nn.sigmoid(s)
    s_hi, s_lo = _split_bf16(s)
    w_hi, w_lo = _split_bf16(w_ref[...])
    n = s.shape[0]
    r = _dot(jnp.concatenate([s_hi, s_lo], axis=0), w_hi)
    o_ref[...] = r[:n] + r[n:] + _dot(s_hi, w_lo) + b_ref[...]


def _mod_vectors(cv, w_mod, b_mod):
    depth, d, n = w_mod.shape
    tn = 1024
    return pl.pallas_call(
        _mod_kernel,
        out_shape=jax.ShapeDtypeStruct((depth, cv.shape[0], n), F32),
        grid=(depth, n // tn),
        in_specs=[
            pl.BlockSpec(cv.shape, lambda l, j: (0, 0)),
            pl.BlockSpec((None, d, tn), lambda l, j: (l, 0, j)),
            pl.BlockSpec((None, 1, tn), lambda l, j: (l, 0, j)),
        ],
        out_specs=pl.BlockSpec((None, cv.shape[0], tn), lambda l, j: (l, 0, j)),
        compiler_params=pltpu.CompilerParams(
            dimension_semantics=("arbitrary", "arbitrary"),
            vmem_limit_bytes=_vmem_limit(5 * d * tn * 4)),
        name="mod_vectors",
    )(cv, w_mod, b_mod.reshape(depth, 1, n))


def _ffn_kernel(x_ref, mod_ref, g_ref, wg_ref, wu_ref, wd_ref, o_ref, h_scr, acc_scr):
    x = x_ref[...]
    h = _rms(x, g_ref[0]) * (1.0 + mod_ref[1]) + mod_ref[0]
    h_scr[...] = h.astype(BF16)
    acc_scr[...] = jnp.zeros_like(acc_scr)

    def chunk(j, carry):
        hb = h_scr[...]
        gate = _dot(hb, wg_ref[j])
        up = _dot(hb, wu_ref[j])
        act = (gate * jax.nn.sigmoid(gate) * up).astype(BF16)
        acc_scr[...] += _dot(act, wd_ref[j])
        return carry

    lax.fori_loop(0, wg_ref.shape[0], chunk, 0)
    y = _rms(acc_scr[...], g_ref[1])
    o_ref[...] = x_ref[...] + 0.5 * mod_ref[2] * y


def _ffn(xt, mod, g2, wg, wu, wd, tm):
    n, d = xt.shape
    nb = n // tm
    per_row = nb // mod.shape[0]
    nc, _, tf = wg.shape
    vmem = 4 * tm * d * 4 + tm * d * 2 + tm * d * 4 + 3 * nc * d * tf * 2 + 4 * tm * tf * 4
    return pl.pallas_call(
        _ffn_kernel,
        out_shape=jax.ShapeDtypeStruct((n, d), F32),
        grid=(nb,),
        in_specs=[
            pl.BlockSpec((tm, d), lambda i: (i, 0)),
            pl.BlockSpec((None, 3, 1, d), lambda i: (i // per_row, 0, 0, 0)),
            pl.BlockSpec((2, 1, d), lambda i: (0, 0, 0)),
            _resident(wg.shape), _resident(wu.shape), _resident(wd.shape),
        ],
        out_specs=pl.BlockSpec((tm, d), lambda i: (i, 0)),
        scratch_shapes=[pltpu.VMEM((tm, d), BF16), pltpu.VMEM((tm, d), F32)],
        compiler_params=pltpu.CompilerParams(
            dimension_semantics=("arbitrary",), vmem_limit_bytes=_vmem_limit(vmem)),
        name="sandwich_ffn",
    )(xt, mod, g2, wg, wu, wd)


def _inproj_kernel(x_ref, mod_ref, g_ref, w_ref, cos_ref, sa_ref, sb_ref, band_ref, icnt_ref,
                   wp_ref, ps_ref, q_ref, k_ref, v_ref, p_ref):
    h = (_rms(x_ref[...], g_ref[0]) * (1.0 + mod_ref[1]) + mod_ref[0]).astype(BF16)
    cos, sa, sb = cos_ref[...], sa_ref[...], sb_ref[...]

    def rope(t):
        up = pltpu.roll(t, V7X_LANES - ROPE_PAIRS, 1)
        down = pltpu.roll(t, ROPE_PAIRS, 1)
        return t * cos + up * sa + down * sb

    scale = HEAD_DIM ** -0.5
    for c in range(NA_WIDTH // V7X_LANES):
        lanes = pl.ds(c * V7X_LANES, V7X_LANES)
        q_ref[:, lanes] = (rope(_dot(h, w_ref[:, pl.ds(c * V7X_LANES, V7X_LANES)])) * scale).astype(BF16)
        k_ref[:, lanes] = rope(_dot(h, w_ref[:, pl.ds(NA_WIDTH + c * V7X_LANES, V7X_LANES)])).astype(BF16)
    v_ref[...] = _dot(h, w_ref[:, pl.ds(2 * NA_WIDTH, NA_WIDTH)]).astype(BF16)
    for g in range(POOL_GROUPS):
        lanes = pl.ds(g * POOL_CH, POOL_CH)
        u = _dot(h, w_ref[:, pl.ds(3 * NA_WIDTH + g * POOL_CH, POOL_CH)])
        u_hi, u_lo = _split_bf16(u)
        band = band_ref[g]
        win = _dot(band, u_hi) + _dot(band, u_lo)
        dlt = (win * icnt_ref[:, lanes] - u).astype(BF16)
        p_ref[:, lanes] = (_dot(dlt, wp_ref[g]) * ps_ref[:, lanes]).astype(BF16)


def _inproj(xt, mod, g1, w_in, rope_tabs, band, icnt, w_pool, pool_scale, tm):
    n, d = xt.shape
    nb = n // tm
    per_row = nb // mod.shape[0]
    tab_blocks = rope_tabs[0].shape[0] // tm
    tab_spec = pl.BlockSpec((tm, V7X_LANES), lambda i: (i % tab_blocks, 0))
    out = jax.ShapeDtypeStruct((n, NA_WIDTH), BF16)
    out_spec = pl.BlockSpec((tm, NA_WIDTH), lambda i: (i, 0))
    vmem = (2 * tm * d * 4 + tm * d * 2 + w_in.size * 2 + 6 * tm * V7X_LANES * 4 + band.size * 2
            + icnt.size * 4 + 8 * tm * NA_WIDTH * 2 + 6 * tm * NA_WIDTH * 4)
    return pl.pallas_call(
        _inproj_kernel,
        out_shape=(out, out, out, out),
        grid=(nb,),
        in_specs=[
            pl.BlockSpec((tm, d), lambda i: (i, 0)),
            pl.BlockSpec((None, 2, 1, d), lambda i: (i // per_row, 0, 0, 0)),
            pl.BlockSpec((1, 1, d), lambda i: (0, 0, 0)),
            _resident(w_in.shape),
            tab_spec, tab_spec, tab_spec,
            _resident(band.shape), _resident(icnt.shape), _resident(w_pool.shape),
            pl.BlockSpec((1, POOL_WIDTH), lambda i: (0, 0)),
        ],
        out_specs=(out_spec, out_spec, out_spec, out_spec),
        compiler_params=pltpu.CompilerParams(
            dimension_semantics=("arbitrary",), vmem_limit_bytes=_vmem_limit(vmem)),
        name="in_projection",
    )(xt, mod, g1, w_in, *rope_tabs, band, icnt, w_pool, pool_scale)


def _softmax_pv(scores, values):
    assert len({s.shape for s in scores}) == 1
    m = jnp.max(functools.reduce(jnp.maximum, scores), axis=-1, keepdims=True)
    probs = [jnp.exp(s - m) for s in scores]
    denom = jnp.sum(functools.reduce(lambda a, b: a + b, probs), axis=-1, keepdims=True)
    o = functools.reduce(lambda a, b: a + b, [_dot(p.astype(BF16), v) for p, v in zip(probs, values)])
    return o / denom


def _na_kernel(q_ref, kp_ref, kc_ref, kn_ref, vp_ref, vc_ref, vn_ref, kx_ref, vx_ref, bias_ref, o_ref):
    for h in range(NA_HEADS):
        cols = pl.ds(h * HEAD_DIM, HEAD_DIM)
        q = q_ref[:, cols]
        scores = [
            _dot_nt(q, k_ref[:, cols]) + bias_ref[h, :, pl.ds(t * ATT_TQ, ATT_TQ)]
            for t, k_ref in enumerate((kp_ref, kc_ref, kn_ref))
        ]
        scores.append(_dot_nt(q, kx_ref[:, cols]))
        values = [v_ref[:, cols] for v_ref in (vp_ref, vc_ref, vn_ref, vx_ref)]
        o_ref[:, cols] = _softmax_pv(scores, values).astype(o_ref.dtype)


def _neighbourhood_attention(q, k, v, kx, vx, bias, batch, ctx_len):
    n = q.shape[0]
    nblk = n // batch // ATT_TQ
    cur = lambda b, i: (b * nblk + i, 0)
    prev = lambda b, i: (b * nblk + jnp.maximum(i - 1, 0), 0)
    nxt = lambda b, i: (b * nblk + jnp.minimum(i + 1, nblk - 1), 0)
    variant = lambda b, i: (jnp.where(i == 0, 0, jnp.where(i == nblk - 1, 2, 1)), 0, 0, 0)
    blk = lambda f: pl.BlockSpec((ATT_TQ, NA_WIDTH), f)
    ctx = pl.BlockSpec((ctx_len, NA_WIDTH), lambda b, i: (b, 0))
    vmem = 2 * bias[0].size * 4 + 16 * ATT_TQ * NA_WIDTH * 2 + 4 * ctx_len * NA_WIDTH * 2 + 16 * ATT_TQ * ATT_TQ * 4
    return pl.pallas_call(
        _na_kernel,
        out_shape=jax.ShapeDtypeStruct((n, NA_WIDTH), BF16),
        grid=(batch, nblk),
        in_specs=[blk(cur), blk(prev), blk(cur), blk(nxt), blk(prev), blk(cur), blk(nxt), ctx, ctx,
                  pl.BlockSpec((None,) + bias.shape[1:], variant)],
        out_specs=blk(cur),
        compiler_params=pltpu.CompilerParams(
            dimension_semantics=("arbitrary", "arbitrary"), vmem_limit_bytes=_vmem_limit(vmem)),
        name="neighbourhood_attention",
    )(q, k, k, k, v, v, v, kx, vx, bias)


def _ctx_attn_kernel(q_ref, k_ref, v_ref, o_ref):
    for h in range(NA_HEADS):
        cols = pl.ds(h * HEAD_DIM, HEAD_DIM)
        s = _dot_nt(q_ref[:, cols], k_ref[:, cols])
        o_ref[:, cols] = _softmax_pv([s], [v_ref[:, cols]]).astype(o_ref.dtype)


def _context_attention(q, k, v, batch):
    n = q.shape[0]
    spec = pl.BlockSpec((n // batch, NA_WIDTH), lambda b: (b, 0))
    return pl.pallas_call(
        _ctx_attn_kernel,
        out_shape=jax.ShapeDtypeStruct((n, NA_WIDTH), BF16),
        grid=(batch,),
        in_specs=[spec, spec, spec],
        out_specs=spec,
        compiler_params=pltpu.CompilerParams(dimension_semantics=("arbitrary",)),
        name="context_attention",
    )(q, k, v)


def _outproj_kernel(x_ref, a_ref, p_ref, w_ref, g_ref, gate_ref, o_ref):
    y = _dot(a_ref[...], w_ref[pl.ds(0, NA_WIDTH), :]) + _dot(p_ref[...], w_ref[pl.ds(NA_WIDTH, POOL_WIDTH), :])
    o_ref[...] = x_ref[...] + gate_ref[0] * _rms(y, g_ref[0])


def _outproj(xt, att, pool, w_out, g1, gate, tm):
    n, d = xt.shape
    nb = n // tm
    per_row = nb // gate.shape[0]
    half = pl.BlockSpec((tm, NA_WIDTH), lambda i: (i, 0))
    vmem = 4 * tm * d * 4 + 4 * tm * NA_WIDTH * 2 + w_out.size * 2 + 2 * tm * d * 4
    return pl.pallas_call(
        _outproj_kernel,
        out_shape=jax.ShapeDtypeStruct((n, d), F32),
        grid=(nb,),
        in_specs=[
            pl.BlockSpec((tm, d), lambda i: (i, 0)), half, half,
            _resident(w_out.shape),
            pl.BlockSpec((1, 1, d), lambda i: (0, 0, 0)),
            pl.BlockSpec((None, 1, 1, d), lambda i: (i // per_row, 0, 0, 0)),
        ],
        out_specs=pl.BlockSpec((tm, d), lambda i: (i, 0)),
        compiler_params=pltpu.CompilerParams(
            dimension_semantics=("arbitrary",), vmem_limit_bytes=_vmem_limit(vmem)),
        name="out_projection",
    )(xt, att, pool, w_out, g1, gate)


def _rope_tables(seq):
    t = np.arange(seq)
    inv = ROPE_THETA ** (-np.arange(ROPE_PAIRS, dtype=np.float64) / ROPE_PAIRS)
    ang_r = (t // GRID_W)[:, None] * inv
    ang_c = (t % GRID_W)[:, None] * inv
    ang = np.concatenate([ang_r, ang_r, ang_c, ang_c], axis=1)
    first = np.tile(np.arange(HEAD_DIM) % (2 * ROPE_PAIRS) < ROPE_PAIRS, 2)
    cos = np.tile(np.cos(ang), (1, 2))
    sin = np.tile(np.sin(ang), (1, 2))
    sa = np.where(first, -sin, 0.0)
    sb = np.where(first, 0.0, sin)
    return tuple(jnp.asarray(a, F32) for a in (cos, sa, sb))


def _identity_rope_tables(n):
    ones = jnp.ones((n, V7X_LANES), F32)
    zeros = jnp.zeros((n, V7X_LANES), F32)
    return ones, zeros, zeros


def _pool_tables(tm, length):
    t = np.arange(tm)
    pos = t % length
    base = t - pos
    bands, icnts = [], []
    for w in POOL_WINDOWS:
        lo = np.clip(pos - w // 2, 0, length)
        hi = np.clip(pos - w // 2 + w, 0, length)
        s = t[None, :]
        bands.append((s >= (base + lo)[:, None]) & (s < (base + hi)[:, None]))
        icnts.append(np.repeat((1.0 / (hi - lo))[:, None], POOL_CH, axis=1))
    band = jnp.asarray(np.stack(bands), BF16)
    icnt = jnp.asarray(np.concatenate(icnts, axis=1), F32)
    return band, icnt


def _attention_bias(rpb, rows):
    heads = rpb.shape[0]
    kh = min(NA_KH, rows)
    assert kh == NA_KH and ATT_ROWS == NA_KH // 2 and rows >= 3 * ATT_ROWS
    by_row = jnp.stack([rpb[:, ATT_ROWS - 1 - i:ATT_ROWS - 1 - i + 3 * ATT_ROWS] for i in range(ATT_ROWS)], axis=1)
    pad = GRID_W - NA_KW
    padded = jnp.pad(by_row, ((0, 0), (0, 0), (0, 0), (pad, pad)))
    toep = jnp.stack([padded[..., GRID_W - 1 - j:2 * GRID_W - 1 - j] for j in range(GRID_W)], axis=3)
    base = jnp.transpose(toep, (0, 1, 3, 2, 4)).reshape(heads, ATT_TQ, 3 * ATT_TQ)

    i = np.arange(ATT_TQ) // GRID_W
    j = np.arange(ATT_TQ) % GRID_W
    m = np.arange(3 * ATT_TQ) // GRID_W
    c = np.arange(3 * ATT_TQ) % GRID_W
    col_start = np.clip(j - NA_KW // 2, 0, GRID_W - NA_KW)
    col_ok = (c[None, :] >= col_start[:, None]) & (c[None, :] < col_start[:, None] + NA_KW)
    masks = []
    for r0 in (0, ATT_ROWS, rows - ATT_ROWS):
        r = r0 + i
        kr = r0 - ATT_ROWS + m
        row_start = np.clip(r - kh // 2, 0, rows - kh)
        row_ok = (kr[None, :] >= row_start[:, None]) & (kr[None, :] < row_start[:, None] + kh)
        masks.append(row_ok & col_ok)
    return jnp.where(np.stack(masks)[:, None], base[None], NEG_INF).astype(F32)


def kernel(x, c, ctx, c_ctx, w_mod, b_mod, norm_g, w_ffn_gate_up, w_ffn_down, w_in, w_out, na_rpb, w_pool, pool_scale):
    batch, seq, d = x.shape
    ctx_len = ctx.shape[1]
    depth = w_mod.shape[0]
    d_ff = w_ffn_down.shape[2]
    rows = seq // GRID_W
    n_chunks = d_ff // FF_CHUNK
    tm_x, tm_c = 512, ctx_len

    cv = jnp.zeros((V7X_SUBLANES, d), F32).at[:batch].set(c).at[batch].set(c_ctx)
    mods = _mod_vectors(cv, w_mod, b_mod)[:, :batch + 1].reshape(depth, batch + 1, N_MOD, 1, d)

    wgu = w_ffn_gate_up.astype(BF16).reshape(depth, 2, d, 2, n_chunks, FF_CHUNK)
    wgu = jnp.transpose(wgu, (0, 1, 3, 4, 2, 5))
    wdn = w_ffn_down.astype(BF16).reshape(depth, 2, n_chunks, FF_CHUNK, d)
    w_in_b, w_out_b, w_pool_b = w_in.astype(BF16), w_out.astype(BF16), w_pool.astype(BF16)
    norm = norm_g.reshape(depth, 6, 1, d)

    rope_x = _rope_tables(seq)
    rope_c = _identity_rope_tables(tm_c)
    band_x, icnt_x = _pool_tables(tm_x, GRID_W)
    band_c, icnt_c = _pool_tables(tm_c, ctx_len)

    xt = x.reshape(batch * seq, d)
    ct = ctx.reshape(batch * ctx_len, d)
    for l in range(depth):
        last = l == depth - 1
        mx, mc = mods[l, :batch], mods[l, batch:]
        g = norm[l]
        ps = pool_scale[l].reshape(1, POOL_WIDTH)
        ffn = lambda t, m, f, tm: _ffn(t, m, g[4 * f:4 * f + 2], wgu[l, f, 0], wgu[l, f, 1], wdn[l, f], tm)
        xt = ffn(xt, mx[:, 0:3], 0, tm_x)
        ct = ffn(ct, mc[:, 0:3], 0, tm_c)
        qx, kx, vx, px = _inproj(xt, mx[:, 3:5], g[2:3], w_in_b[l], rope_x, band_x, icnt_x, w_pool_b[l], ps, tm_x)
        qc, kc, vc, pc = _inproj(ct, mc[:, 3:5], g[2:3], w_in_b[l], rope_c, band_c, icnt_c, w_pool_b[l], ps, tm_c)
        bias = _attention_bias(na_rpb[l], rows)
        ax = _neighbourhood_attention(qx, kx, vx, kc, vc, bias, batch, ctx_len)
        xt = _outproj(xt, ax, px, w_out_b[l], g[3:4], mx[:, 5:6], tm_x)
        xt = ffn(xt, mx[:, 6:9], 1, tm_x)
        if not last:
            ac = _context_attention(qc, kc, vc, batch)
            ct = _outproj(ct, ac, pc, w_out_b[l], g[3:4], mc[:, 5:6], tm_c)
            ct = ffn(ct, mc[:, 6:9], 1, tm_c)
    return xt.reshape(batch, seq, d)
```

```python
import functools

import numpy as np
import jax
import jax.numpy as jnp
from jax import lax
from jax.experimental import pallas as pl
from jax.experimental.pallas import tpu as pltpu

GRID_W = 64
N_MOD = 9
NA_HEADS = 8
HEAD_DIM = 64
NA_WIDTH = NA_HEADS * HEAD_DIM
NA_KH = 8
NA_KW = 16
POOL_GROUPS = 4
POOL_CH = 128
POOL_WIDTH = POOL_GROUPS * POOL_CH
POOL_WINDOWS = (2, 4, 8, 16)
ROPE_THETA = 10000.0
ROPE_PAIRS = HEAD_DIM // 4
RMS_EPS = 1e-6
NEG_INF = -1e30
LOG2_E = 1.4426950408889634

V7X_LANES = 128
V7X_SUBLANES = 8
V7X_VMEM_BYTES = 64 * 1024 * 1024

FF_CHUNK = 256
ATT_ROWS = 4
ATT_TQ = ATT_ROWS * GRID_W

F32 = jnp.float32
BF16 = jnp.bfloat16


def _vmem_limit(nbytes):
    return int(min(nbytes + (8 << 20), V7X_VMEM_BYTES - (6 << 20)))


def _resident(block_shape):
    zeros = (0,) * len(block_shape)
    return pl.BlockSpec(block_shape, lambda *_: zeros, pipeline_mode=pl.Buffered(1))


def _rms(x, g):
    return x * lax.rsqrt(jnp.mean(x * x, axis=-1, keepdims=True) + RMS_EPS) * g


def _dot(a, b):
    return jnp.dot(a, b, preferred_element_type=F32)


def _dot_nt(a, b):
    return lax.dot_general(a, b, (((1,), (1,)), ((), ())), preferred_element_type=F32)


def _split_bf16(x):
    hi = x.astype(BF16)
    lo = (x - hi.astype(F32)).astype(BF16)
    return hi, lo


def _mod_kernel(cv_ref, w_ref, b_ref, o_ref):
    s = cv_ref[...]
    s = s * jax.nn.sigmoid(s)
    s_hi, s_lo = _split_bf16(s)
    w_hi, w_lo = _split_bf16(w_ref[...])
    n = s.shape[0]
    r = _dot(jnp.concatenate([s_hi, s_lo], axis=0), w_hi)
    o_ref[...] = r[:n] + r[n:] + _dot(s_hi, w_lo) + b_ref[...]


def _mod_vectors(cv, w_mod, b_mod):
    depth, d, n = w_mod.shape
    tn = 1024
    return pl.pallas_call(
        _mod_kernel,
        out_shape=jax.ShapeDtypeStruct((depth, cv.shape[0], n), F32),
        grid=(depth, n // tn),
        in_specs=[
            pl.BlockSpec(cv.shape, lambda l, j: (0, 0)),
            pl.BlockSpec((None, d, tn), lambda l, j: (l, 0, j)),
            pl.BlockSpec((None, 1, tn), lambda l, j: (l, 0, j)),
        ],
        out_specs=pl.BlockSpec((None, cv.shape[0], tn), lambda l, j: (l, 0, j)),
        compiler_params=pltpu.CompilerParams(
            dimension_semantics=("arbitrary", "arbitrary"),
            vmem_limit_bytes=_vmem_limit(5 * d * tn * 4)),
        name="mod_vectors",
    )(cv, w_mod, b_mod.reshape(depth, 1, n))


def _ffn_kernel(x_ref, mod_ref, g_ref, wg_ref, wu_ref, wd_ref, o_ref, h_scr, acc_scr):
    x = x_ref[...]
    h = _rms(x, g_ref[0]) * (1.0 + mod_ref[1]) + mod_ref[0]
    h_scr[...] = h.astype(BF16)
    for j in range(wg_ref.shape[0]):
        hb = h_scr[...]
        gate = _dot(hb, wg_ref[j])
        up = _dot(hb, wu_ref[j])
        act = (gate * jax.nn.sigmoid(gate) * up).astype(BF16)
        part = _dot(act, wd_ref[j])
        if j == 0:
            acc_scr[...] = part
        else:
            acc_scr[...] += part
    y = _rms(acc_scr[...], g_ref[1])
    o_ref[...] = x_ref[...] + 0.5 * mod_ref[2] * y


def _ffn(xt, mod, g2, wg, wu, wd, tm):
    n, d = xt.shape
    nb = n // tm
    per_row = nb // mod.shape[0]
    nc, _, tf = wg.shape
    vmem = 4 * tm * d * 4 + tm * d * 2 + tm * d * 4 + 3 * nc * d * tf * 2 + 4 * tm * tf * 4
    return pl.pallas_call(
        _ffn_kernel,
        out_shape=jax.ShapeDtypeStruct((n, d), F32),
        grid=(nb,),
        in_specs=[
            pl.BlockSpec((tm, d), lambda i: (i, 0)),
            pl.BlockSpec((None, 3, 1, d), lambda i: (i // per_row, 0, 0, 0)),
            pl.BlockSpec((2, 1, d), lambda i: (0, 0, 0)),
            _resident(wg.shape), _resident(wu.shape), _resident(wd.shape),
        ],
        out_specs=pl.BlockSpec((tm, d), lambda i: (i, 0)),
        scratch_shapes=[pltpu.VMEM((tm, d), BF16), pltpu.VMEM((tm, d), F32)],
        compiler_params=pltpu.CompilerParams(
            dimension_semantics=("arbitrary",), vmem_limit_bytes=_vmem_limit(vmem)),
        name="sandwich_ffn",
    )(xt, mod, g2, wg, wu, wd)


def _inproj_kernel(x_ref, mod_ref, g_ref, w_ref, cos_ref, sa_ref, sb_ref, band_ref, icnt_ref,
                   wp_ref, ps_ref, q_ref, k_ref, v_ref, p_ref):
    h = (_rms(x_ref[...], g_ref[0]) * (1.0 + mod_ref[1]) + mod_ref[0]).astype(BF16)
    cos, sa, sb = cos_ref[...], sa_ref[...], sb_ref[...]

    def rope(t):
        up = pltpu.roll(t, V7X_LANES - ROPE_PAIRS, 1)
        down = pltpu.roll(t, ROPE_PAIRS, 1)
        return t * cos + up * sa + down * sb

    scale = HEAD_DIM ** -0.5 * LOG2_E
    qk =_dot(h, w_ref[:, pl.ds(0, 2 * NA_WIDTH)])
    for c in range(NA_WIDTH // V7X_LANES):
        lanes = pl.ds(c * V7X_LANES, V7X_LANES)
        q_ref[:, lanes] = (rope(qk[:, c * V7X_LANES:(c + 1) * V7X_LANES]) * scale).astype(BF16)
        k_ref[:, lanes] = rope(qk[:, NA_WIDTH + c * V7X_LANES:NA_WIDTH + (c + 1) * V7X_LANES]).astype(BF16)
    vu = _dot(h, w_ref[:, pl.ds(2 * NA_WIDTH, NA_WIDTH + POOL_WIDTH)])
    v_ref[...] = vu[:, :NA_WIDTH].astype(BF16)
    tm, tb = x_ref.shape[0], band_ref.shape[1]
    for g in range(POOL_GROUPS):
        lanes = pl.ds(g * POOL_CH, POOL_CH)
        u = vu[:, NA_WIDTH + g * POOL_CH:NA_WIDTH + (g + 1) * POOL_CH]
        u2 = jnp.concatenate(_split_bf16(u), axis=1)
        band = band_ref[g]
        win2 = jnp.concatenate([_dot(band, u2[s * tb:(s + 1) * tb]) for s in range(tm // tb)], axis=0)
        win = win2[:, :POOL_CH] + win2[:, POOL_CH:]
        dlt = (win * icnt_ref[:, lanes] - u).astype(BF16)
        p_ref[:, lanes] = (_dot(dlt, wp_ref[g]) * ps_ref[:, lanes]).astype(BF16)


def _inproj(xt, mod, g1, w_in, rope_tabs, band, icnt, w_pool, pool_scale, tm):
    n, d = xt.shape
    nb = n // tm
    per_row = nb // mod.shape[0]
    tab_blocks = rope_tabs[0].shape[0] // tm
    tab_spec = pl.BlockSpec((tm, V7X_LANES), lambda i: (i % tab_blocks, 0))
    out = jax.ShapeDtypeStruct((n, NA_WIDTH), BF16)
    out_spec = pl.BlockSpec((tm, NA_WIDTH), lambda i: (i, 0))
    vmem = (2 * tm * d * 4 + tm * d * 2 + w_in.size * 2 + 6 * tm * V7X_LANES * 4 + band.size * 2
            + icnt.size * 4 + 8 * tm * NA_WIDTH * 2 + 6 * tm * NA_WIDTH * 4)
    return pl.pallas_call(
        _inproj_kernel,
        out_shape=(out, out, out, out),
        grid=(nb,),
        in_specs=[
            pl.BlockSpec((tm, d), lambda i: (i, 0)),
            pl.BlockSpec((None, 2, 1, d), lambda i: (i // per_row, 0, 0, 0)),
            pl.BlockSpec((1, 1, d), lambda i: (0, 0, 0)),
            _resident(w_in.shape),
            tab_spec, tab_spec, tab_spec,
            _resident(band.shape), _resident(icnt.shape), _resident(w_pool.shape),
            pl.BlockSpec((1, POOL_WIDTH), lambda i: (0, 0)),
        ],
        out_specs=(out_spec, out_spec, out_spec, out_spec),
        compiler_params=pltpu.CompilerParams(
            dimension_semantics=("arbitrary",), vmem_limit_bytes=_vmem_limit(vmem)),
        name="in_projection",
    )(xt, mod, g1, w_in, *rope_tabs, band, icnt, w_pool, pool_scale)


def _softmax_pv(scores, values):
    assert len({s.shape for s in scores}) == 1
    m = jnp.max(functools.reduce(jnp.maximum, scores), axis=-1, keepdims=True)
    probs = [jnp.exp2(s - m) for s in scores]
    denom = jnp.sum(functools.reduce(lambda a, b: a + b, probs), axis=-1, keepdims=True)
    o = functools.reduce(lambda a, b: a + b, [_dot(p.astype(BF16), v) for p, v in zip(probs, values)])
    return o / denom


def _na_kernel(q_ref, kp_ref, kc_ref, kn_ref, vp_ref, vc_ref, vn_ref, kx_ref, vx_ref, bias_ref, o_ref):
    for h in range(NA_HEADS):
        cols = pl.ds(h * HEAD_DIM, HEAD_DIM)
        q = q_ref[:, cols]
        scores = [
            _dot_nt(q, k_ref[:, cols]) + bias_ref[h, :, pl.ds(t * ATT_TQ, ATT_TQ)]
            for t, k_ref in enumerate((kp_ref, kc_ref, kn_ref))
        ]
        scores.append(_dot_nt(q, kx_ref[:, cols]))
        values = [v_ref[:, cols] for v_ref in (vp_ref, vc_ref, vn_ref, vx_ref)]
        o_ref[:, cols] = _softmax_pv(scores, values).astype(o_ref.dtype)


def _neighbourhood_attention(q, k, v, kx, vx, bias, batch, ctx_len):
    n = q.shape[0]
    nblk = n // batch // ATT_TQ
    cur = lambda b, i: (b * nblk + i, 0)
    prev = lambda b, i: (b * nblk + jnp.maximum(i - 1, 0), 0)
    nxt = lambda b, i: (b * nblk + jnp.minimum(i + 1, nblk - 1), 0)
    variant = lambda b, i: (jnp.where(i == 0, 0, jnp.where(i == nblk - 1, 2, 1)), 0, 0, 0)
    blk = lambda f: pl.BlockSpec((ATT_TQ, NA_WIDTH), f)
    ctx = pl.BlockSpec((ctx_len, NA_WIDTH), lambda b, i: (b, 0))
    vmem = 2 * bias[0].size * 4 + 16 * ATT_TQ * NA_WIDTH * 2 + 4 * ctx_len * NA_WIDTH * 2 + 16 * ATT_TQ * ATT_TQ * 4
    return pl.pallas_call(
        _na_kernel,
        out_shape=jax.ShapeDtypeStruct((n, NA_WIDTH), BF16),
        grid=(batch, nblk),
        in_specs=[blk(cur), blk(prev), blk(cur), blk(nxt), blk(prev), blk(cur), blk(nxt), ctx, ctx,
                  pl.BlockSpec((None,) + bias.shape[1:], variant)],
        out_specs=blk(cur),
        compiler_params=pltpu.CompilerParams(
            dimension_semantics=("arbitrary", "arbitrary"), vmem_limit_bytes=_vmem_limit(vmem)),
        name="neighbourhood_attention",
    )(q, k, k, k, v, v, v, kx, vx, bias)


def _ctx_attn_kernel(q_ref, k_ref, v_ref, o_ref):
    for h in range(NA_HEADS):
        cols = pl.ds(h * HEAD_DIM, HEAD_DIM)
        s = _dot_nt(q_ref[:, cols], k_ref[:, cols])
        o_ref[:, cols] = _softmax_pv([s], [v_ref[:, cols]]).astype(o_ref.dtype)


def _context_attention(q, k, v, batch):
    n = q.shape[0]
    spec = pl.BlockSpec((n // batch, NA_WIDTH), lambda b: (b, 0))
    return pl.pallas_call(
        _ctx_attn_kernel,
        out_shape=jax.ShapeDtypeStruct((n, NA_WIDTH), BF16),
        grid=(batch,),
        in_specs=[spec, spec, spec],
        out_specs=spec,
        compiler_params=pltpu.CompilerParams(dimension_semantics=("arbitrary",)),
        name="context_attention",
    )(q, k, v)


def _outproj_kernel(x_ref, a_ref, p_ref, w_ref, g_ref, gate_ref, o_ref):
    y = _dot(a_ref[...], w_ref[pl.ds(0, NA_WIDTH), :]) + _dot(p_ref[...], w_ref[pl.ds(NA_WIDTH, POOL_WIDTH), :])
    o_ref[...] = x_ref[...] + gate_ref[0] * _rms(y, g_ref[0])


def _outproj(xt, att, pool, w_out, g1, gate, tm):
    n, d = xt.shape
    nb = n // tm
    per_row = nb // gate.shape[0]
    half = pl.BlockSpec((tm, NA_WIDTH), lambda i: (i, 0))
    vmem = 4 * tm * d * 4 + 4 * tm * NA_WIDTH * 2 + w_out.size * 2 + 2 * tm * d * 4
    return pl.pallas_call(
        _outproj_kernel,
        out_shape=jax.ShapeDtypeStruct((n, d), F32),
        grid=(nb,),
        in_specs=[
            pl.BlockSpec((tm, d), lambda i: (i, 0)), half, half,
            _resident(w_out.shape),
            pl.BlockSpec((1, 1, d), lambda i: (0, 0, 0)),
            pl.BlockSpec((None, 1, 1, d), lambda i: (i // per_row, 0, 0, 0)),
        ],
        out_specs=pl.BlockSpec((tm, d), lambda i: (i, 0)),
        compiler_params=pltpu.CompilerParams(
            dimension_semantics=("arbitrary",), vmem_limit_bytes=_vmem_limit(vmem)),
        name="out_projection",
    )(xt, att, pool, w_out, g1, gate)


def _rope_tables(seq):
    t = np.arange(seq)
    inv = ROPE_THETA ** (-np.arange(ROPE_PAIRS, dtype=np.float64) / ROPE_PAIRS)
    ang_r = (t // GRID_W)[:, None] * inv
    ang_c = (t % GRID_W)[:, None] * inv
    ang = np.concatenate([ang_r, ang_r, ang_c, ang_c], axis=1)
    first = np.tile(np.arange(HEAD_DIM) % (2 * ROPE_PAIRS) < ROPE_PAIRS, 2)
    cos = np.tile(np.cos(ang), (1, 2))
    sin = np.tile(np.sin(ang), (1, 2))
    sa = np.where(first, -sin, 0.0)
    sb = np.where(first, 0.0, sin)
    return tuple(jnp.asarray(a, F32) for a in (cos, sa, sb))


def _identity_rope_tables(n):
    ones = jnp.ones((n, V7X_LANES), F32)
    zeros = jnp.zeros((n, V7X_LANES), F32)
    return ones, zeros, zeros


def _pool_tables(tm, tb, length):
    assert tb % length == 0 and tm % tb == 0
    t = np.arange(tm)
    pos = t % length
    base = t - pos
    bands, icnts = [], []
    for w in POOL_WINDOWS:
        lo = np.clip(pos - w // 2, 0, length)
        hi = np.clip(pos - w // 2 + w, 0, length)
        s = t[None, :tb]
        bands.append((s >= (base + lo)[:tb, None]) & (s < (base + hi)[:tb, None]))
        icnts.append(np.repeat((1.0 / (hi - lo))[:, None], POOL_CH, axis=1))
    band = jnp.asarray(np.stack(bands), BF16)
    icnt = jnp.asarray(np.concatenate(icnts, axis=1), F32)
    return band, icnt


def _attention_bias(rpb, rows):
    heads = rpb.shape[0]
    kh = min(NA_KH, rows)
    assert kh == NA_KH and ATT_ROWS == NA_KH // 2 and rows >= 3 * ATT_ROWS
    by_row = jnp.stack([rpb[:, ATT_ROWS - 1 - i:ATT_ROWS - 1 - i + 3 * ATT_ROWS] for i in range(ATT_ROWS)], axis=1)
    pad = GRID_W - NA_KW
    padded = jnp.pad(by_row, ((0, 0), (0, 0), (0, 0), (pad, pad)))
    toep = jnp.stack([padded[..., GRID_W - 1 - j:2 * GRID_W - 1 - j] for j in range(GRID_W)], axis=3)
    base = jnp.transpose(toep, (0, 1, 3, 2, 4)).reshape(heads, ATT_TQ, 3 * ATT_TQ)

    i = np.arange(ATT_TQ) // GRID_W
    j = np.arange(ATT_TQ) % GRID_W
    m = np.arange(3 * ATT_TQ) // GRID_W
    c = np.arange(3 * ATT_TQ) % GRID_W
    col_start = np.clip(j - NA_KW // 2, 0, GRID_W - NA_KW)
    col_ok = (c[None, :] >= col_start[:, None]) & (c[None, :] < col_start[:, None] + NA_KW)
    masks = []
    for r0 in (0, ATT_ROWS, rows - ATT_ROWS):
        r = r0 + i
        kr = r0 - ATT_ROWS + m
        row_start = np.clip(r - kh // 2, 0, rows - kh)
        row_ok = (kr[None, :] >= row_start[:, None]) & (kr[None, :] < row_start[:, None] + kh)
        masks.append(row_ok & col_ok)
    return jnp.where(np.stack(masks)[:, None], base[None] * LOG2_E, NEG_INF).astype(F32)


def kernel(x, c, ctx, c_ctx, w_mod, b_mod, norm_g, w_ffn_gate_up, w_ffn_down, w_in, w_out, na_rpb, w_pool, pool_scale):
    batch, seq, d = x.shape
    ctx_len = ctx.shape[1]
    depth = w_mod.shape[0]
    d_ff = w_ffn_down.shape[2]
    rows = seq // GRID_W
    n_chunks = d_ff // FF_CHUNK
    tm_x, tm_c = 512, ctx_len

    cv = jnp.zeros((V7X_SUBLANES, d), F32).at[:batch].set(c).at[batch].set(c_ctx)
    mods = _mod_vectors(cv, w_mod, b_mod)[:, :batch + 1].reshape(depth, batch + 1, N_MOD, 1, d)

    wgu = w_ffn_gate_up.astype(BF16).reshape(depth, 2, d, 2, n_chunks, FF_CHUNK)
    wgu = jnp.transpose(wgu, (0, 1, 3, 4, 2, 5))
    wdn = w_ffn_down.astype(BF16).reshape(depth, 2, n_chunks, FF_CHUNK, d)
    w_in_b, w_out_b, w_pool_b = w_in.astype(BF16), w_out.astype(BF16), w_pool.astype(BF16)
    norm = norm_g.reshape(depth, 6, 1, d)

    rope_x = _rope_tables(seq)
    rope_c = _identity_rope_tables(tm_c)
    band_x, icnt_x = _pool_tables(tm_x, 2 * GRID_W, GRID_W)
    band_c, icnt_c = _pool_tables(tm_c, ctx_len, ctx_len)

    xt = x.reshape(batch * seq, d)
    ct = ctx.reshape(batch * ctx_len, d)
    for l in range(depth):
        last = l == depth - 1
        mx, mc = mods[l, :batch], mods[l, batch:]
        g = norm[l]
        ps = pool_scale[l].reshape(1, POOL_WIDTH)
        ffn = lambda t, m, f, tm: _ffn(t, m, g[4 * f:4 * f + 2], wgu[l, f, 0], wgu[l, f, 1], wdn[l, f], tm)
        xt = ffn(xt, mx[:, 0:3], 0, tm_x)
        ct = ffn(ct, mc[:, 0:3], 0, tm_c)
        qx, kx, vx, px = _inproj(xt, mx[:, 3:5], g[2:3], w_in_b[l], rope_x, band_x, icnt_x, w_pool_b[l], ps, tm_x)
        qc, kc, vc, pc = _inproj(ct, mc[:, 3:5], g[2:3], w_in_b[l], rope_c, band_c, icnt_c, w_pool_b[l], ps, tm_c)
        bias = _attention_bias(na_rpb[l], rows)
        ax = _neighbourhood_attention(qx, kx, vx, kc, vc, bias, batch, ctx_len)
        xt = _outproj(xt, ax, px, w_out_b[l], g[3:4], mx[:, 5:6], tm_x)
        xt = ffn(xt, mx[:, 6:9], 1, tm_x)
        if not last:
            ac = _context_attention(qc, kc, vc, batch)
            ct = _outproj(ct, ac, pc, w_out_b[l], g[3:4], mc[:, 5:6], tm_c)
            ct = ffn(ct, mc[:, 6:9], 1, tm_c)
    return xt.reshape(batch, seq, d)
```

```python
import functools

import numpy as np
import jax
import jax.numpy as jnp
from jax import lax
from jax.experimental import pallas as pl
from jax.experimental.pallas import tpu as pltpu

GRID_W = 64
N_MOD = 9
NA_HEADS = 8
HEAD_DIM = 64
NA_WIDTH = NA_HEADS * HEAD_DIM
NA_KH = 8
NA_KW = 16
POOL_GROUPS = 4
POOL_CH = 128
POOL_WIDTH = POOL_GROUPS * POOL_CH
POOL_WINDOWS = (2, 4, 8, 16)
ROPE_THETA = 10000.0
ROPE_PAIRS = HEAD_DIM // 4
RMS_EPS = 1e-6
NEG_INF = -1e30
LOG2_E = 1.4426950408889634

V7X_LANES = 128
V7X_SUBLANES = 8
V7X_VMEM_BYTES = 64 * 1024 * 1024

FF_CHUNK = 256
ATT_ROWS = 4
ATT_TQ = ATT_ROWS * GRID_W

F32 = jnp.float32
BF16 = jnp.bfloat16


def _vmem_limit(nbytes):
    return int(min(nbytes + (8 << 20), V7X_VMEM_BYTES - (6 << 20)))


def _resident(block_shape, lead=()):
    index = tuple(lead) + (0,) * len(block_shape)
    return pl.BlockSpec((None,) * len(lead) + tuple(block_shape), lambda *_: index, pipeline_mode=pl.Buffered(1))


def _rms(x, g):
    return x * lax.rsqrt(jnp.mean(x * x, axis=-1, keepdims=True) + RMS_EPS) * g


def _dot(a, b):
    return jnp.dot(a, b, preferred_element_type=F32)


def _split_bf16(x):
    hi = x.astype(BF16)
    lo = (x - hi.astype(F32)).astype(BF16)
    return hi, lo


def _mod_kernel(cv_ref, w_ref, b_ref, o_ref):
    s = cv_ref[...]
    s = s * jax.nn.sigmoid(s)
    s_hi, s_lo = _split_bf16(s)
    w_hi, w_lo = _split_bf16(w_ref[...])
    n = s.shape[0]
    r = _dot(jnp.concatenate([s_hi, s_lo], axis=0), w_hi)
    o_ref[...] = r[:n] + r[n:] + _dot(s_hi, w_lo) + b_ref[...]


def _mod_vectors(cv, w_mod, b_mod):
    depth, d, n = w_mod.shape
    tn = 1024
    return pl.pallas_call(
        _mod_kernel,
        out_shape=jax.ShapeDtypeStruct((depth, cv.shape[0], n), F32),
        grid=(depth, n // tn),
        in_specs=[
            pl.BlockSpec(cv.shape, lambda l, j: (0, 0)),
            pl.BlockSpec((None, d, tn), lambda l, j: (l, 0, j)),
            pl.BlockSpec((None, 1, tn), lambda l, j: (l, 0, j)),
        ],
        out_specs=pl.BlockSpec((None, cv.shape[0], tn), lambda l, j: (l, 0, j)),
        compiler_params=pltpu.CompilerParams(
            dimension_semantics=("arbitrary", "arbitrary"),
            vmem_limit_bytes=_vmem_limit(5 * d * tn * 4)),
        name="mod_vectors",
    )(cv, w_mod, b_mod.reshape(depth, 1, n))


def _ffn_kernel(*refs, mixed):
    if mixed:
        x_ref, a_ref, p_ref, wo_ref, mod_ref, g_ref, wgu_ref, wd_ref, o_ref, h_scr, acc_scr = refs
        att = jnp.concatenate([a_ref[c] for c in range(a_ref.shape[0])], axis=1)
        y = _dot(att, wo_ref[pl.ds(0, NA_WIDTH), :]) + _dot(p_ref[...], wo_ref[pl.ds(NA_WIDTH, POOL_WIDTH), :])
        o_ref[...] = x_ref[...] + mod_ref[0] * _rms(y, g_ref[0])
        x_ref, mod_ref, g_ref = o_ref, mod_ref.at[pl.ds(1, 3)], g_ref.at[pl.ds(1, 2)]
    else:
        x_ref, mod_ref, g_ref, wgu_ref, wd_ref, o_ref, h_scr, acc_scr = refs
    h = _rms(x_ref[...], g_ref[0]) * (1.0 + mod_ref[1]) + mod_ref[0]
    h_scr[...] = h.astype(BF16)
    d_ff = wd_ref.shape[0]
    for j in range(d_ff // FF_CHUNK):
        hb = h_scr[...]
        gate = _dot(hb, wgu_ref[:, pl.ds(j * FF_CHUNK, FF_CHUNK)])
        up = _dot(hb, wgu_ref[:, pl.ds(d_ff + j * FF_CHUNK, FF_CHUNK)])
        act = (gate * jax.nn.sigmoid(gate) * up).astype(BF16)
        part = _dot(act, wd_ref[pl.ds(j * FF_CHUNK, FF_CHUNK), :])
        if j == 0:
            acc_scr[...] = part
        else:
            acc_scr[...] += part
    y = _rms(acc_scr[...], g_ref[1])
    o_ref[...] = x_ref[...] + 0.5 * mod_ref[2] * y


def _ffn(xt, mod, g, wgu, wd, which, tm, mix=None):
    n, d = xt.shape
    nb = n // tm
    per_row = nb // mod.shape[0]
    d_ff = wd.shape[2]
    assert d_ff % FF_CHUNK == 0 and wgu.shape[2:] == (d, 2 * d_ff)
    tok = pl.BlockSpec((tm, d), lambda i: (i, 0))
    vmem = 4 * tm * d * 4 + tm * d * 2 + tm * d * 4 + 3 * d * d_ff * 2 + 4 * tm * FF_CHUNK * 4
    operands, specs = [xt], [tok]
    if mix is not None:
        att, pool, w_out = mix
        operands += [att, pool, w_out]
        specs += [pl.BlockSpec((att.shape[0], tm, att.shape[2]), lambda i: (0, i, 0)),
                  pl.BlockSpec((tm, pool.shape[1]), lambda i: (i, 0)), _resident(w_out.shape)]
        vmem += 4 * tm * pool.shape[1] * 2 + w_out.size * 2 + 2 * tm * d * 4
    operands += [mod, g, wgu, wd]
    specs += [
        pl.BlockSpec((None,) + mod.shape[1:], lambda i: (i // per_row, 0, 0, 0)),
        pl.BlockSpec(g.shape, lambda i: (0, 0, 0)),
        _resident(wgu.shape[2:], which), _resident(wd.shape[2:], which),
    ]
    return pl.pallas_call(
        functools.partial(_ffn_kernel, mixed=mix is not None),
        out_shape=jax.ShapeDtypeStruct((n, d), F32),
        grid=(nb,),
        in_specs=specs,
        out_specs=tok,
        scratch_shapes=[pltpu.VMEM((tm, d), BF16), pltpu.VMEM((tm, d), F32)],
        compiler_params=pltpu.CompilerParams(
            dimension_semantics=("arbitrary",), vmem_limit_bytes=_vmem_limit(vmem)),
        name="mix_ffn" if mix is not None else "sandwich_ffn",
    )(*operands)


def _inproj_kernel(x_ref, mod_ref, g_ref, w_ref, cos_ref, sa_ref, sb_ref, band_ref, icnt_ref,
                   wp_ref, ps_ref, q_ref, kt_ref, v_ref, p_ref):
    h = (_rms(x_ref[...], g_ref[0]) * (1.0 + mod_ref[1]) + mod_ref[0]).astype(BF16)
    cos, sa, sb = cos_ref[...], sa_ref[...], sb_ref[...]

    def rope(t):
        up = pltpu.roll(t, V7X_LANES - ROPE_PAIRS, 1)
        down = pltpu.roll(t, ROPE_PAIRS, 1)
        return t * cos + up * sa + down * sb

    scale = HEAD_DIM ** -0.5 * LOG2_E
    qk =_dot(h, w_ref[:, pl.ds(0, 2 * NA_WIDTH)])
    for c in range(NA_WIDTH // V7X_LANES):
        lanes = pl.ds(c * V7X_LANES, V7X_LANES)
        q_ref[c] = (rope(qk[:, c * V7X_LANES:(c + 1) * V7X_LANES]) * scale).astype(BF16)
        k_rot = rope(qk[:, NA_WIDTH + c * V7X_LANES:NA_WIDTH + (c + 1) * V7X_LANES])
        kt_ref[lanes, :] = k_rot.T.astype(BF16)
    vu = _dot(h, w_ref[:, pl.ds(2 * NA_WIDTH, NA_WIDTH + POOL_WIDTH)])
    for c in range(NA_WIDTH // V7X_LANES):
        v_ref[c] = vu[:, c * V7X_LANES:(c + 1) * V7X_LANES].astype(BF16)
    tm, tb = x_ref.shape[0], band_ref.shape[1]
    for g in range(POOL_GROUPS):
        lanes = pl.ds(g * POOL_CH, POOL_CH)
        u = vu[:, NA_WIDTH + g * POOL_CH:NA_WIDTH + (g + 1) * POOL_CH]
        u2 = jnp.concatenate(_split_bf16(u), axis=1)
        band = band_ref[g]
        win2 = jnp.concatenate([_dot(band, u2[s * tb:(s + 1) * tb]) for s in range(tm // tb)], axis=0)
        win = win2[:, :POOL_CH] + win2[:, POOL_CH:]
        dlt = (win * icnt_ref[:, lanes] - u).astype(BF16)
        p_ref[:, lanes] = (_dot(dlt, wp_ref[g]) * ps_ref[:, lanes]).astype(BF16)


def _inproj(xt, mod, g1, w_in, rope_tabs, band, icnt, w_pool, pool_scale, tm):
    n, d = xt.shape
    nb = n // tm
    per_row = nb // mod.shape[0]
    tab_blocks = rope_tabs[0].shape[0] // tm
    tab_spec = pl.BlockSpec((tm, V7X_LANES), lambda i: (i % tab_blocks, 0))
    out = jax.ShapeDtypeStruct((n, NA_WIDTH), BF16)
    out_spec = pl.BlockSpec((tm, NA_WIDTH), lambda i: (i, 0))
    vmem = (2 * tm * d * 4 + tm * d * 2 + w_in.size * 2 + 6 * tm * V7X_LANES * 4 + band.size * 2
            + icnt.size * 4 + 8 * tm * NA_WIDTH * 2 + 6 * tm * NA_WIDTH * 4)
    out_t = jax.ShapeDtypeStruct((NA_WIDTH, n), BF16)
    out_t_spec = pl.BlockSpec((NA_WIDTH, tm), lambda i: (0, i))
    pairs = NA_WIDTH // V7X_LANES
    out_p = jax.ShapeDtypeStruct((pairs, n, V7X_LANES), BF16)
    out_p_spec = pl.BlockSpec((pairs, tm, V7X_LANES), lambda i: (0, i, 0))
    return pl.pallas_call(
        _inproj_kernel,
        out_shape=(out_p, out_t, out_p, out),
        grid=(nb,),
        in_specs=[
            pl.BlockSpec((tm, d), lambda i: (i, 0)),
            pl.BlockSpec((None, 2, 1, d), lambda i: (i // per_row, 0, 0, 0)),
            pl.BlockSpec((1, 1, d), lambda i: (0, 0, 0)),
            _resident(w_in.shape),
            tab_spec, tab_spec, tab_spec,
            _resident(band.shape), _resident(icnt.shape), _resident(w_pool.shape),
            pl.BlockSpec((1, POOL_WIDTH), lambda i: (0, 0)),
        ],
        out_specs=(out_p_spec, out_t_spec, out_p_spec, out_spec),
        compiler_params=pltpu.CompilerParams(
            dimension_semantics=("arbitrary",), vmem_limit_bytes=_vmem_limit(vmem)),
        name="in_projection",
    )(xt, mod, g1, w_in, *rope_tabs, band, icnt, w_pool, pool_scale)


def _softmax_pv(scores, values):
    assert len({s.shape for s in scores}) == 1
    m = jnp.max(functools.reduce(jnp.maximum, scores), axis=-1, keepdims=True)
    probs = [jnp.exp2(s - m) for s in scores]
    denom = jnp.sum(functools.reduce(lambda a, b: a + b, probs), axis=-1, keepdims=True)
    o = functools.reduce(lambda a, b: a + b, [_dot(p.astype(BF16), v) for p, v in zip(probs, values)])
    return o / denom


HEAD_PAIRS = NA_WIDTH // V7X_LANES


def _na_kernel(q_ref, kp_ref, kc_ref, kn_ref, vp_ref, vc_ref, vn_ref, kx_ref, vx_ref, bias_ref, o_ref):
    kt_refs = (kp_ref, kc_ref, kn_ref, kx_ref)
    v_refs = (vp_ref, vc_ref, vn_ref, vx_ref)
    n_loc = 3 * ATT_TQ
    for p in range(HEAD_PAIRS):
        outs = []
        for hh in range(2):
            h = 2 * p + hh
            lanes = slice(hh * HEAD_DIM, (hh + 1) * HEAD_DIM)
            feat = pl.ds(h * HEAD_DIM, HEAD_DIM)
            kt = jnp.concatenate([r[feat, :] for r in kt_refs], axis=1)
            val = jnp.concatenate([r[p][:, lanes] for r in v_refs], axis=0)
            s = _dot(q_ref[p][:, lanes], kt)
            s = jnp.concatenate([s[:, :n_loc] + bias_ref[h], s[:, n_loc:]], axis=1)
            m = jnp.max(s, axis=-1, keepdims=True)
            e = jnp.exp2(s - m)
            denom = jnp.sum(e, axis=-1, keepdims=True)
            outs.append(_dot(e.astype(BF16), val) / denom)
        o_ref[p] = jnp.concatenate(outs, axis=1).astype(o_ref.dtype)


def _neighbourhood_attention(q, kt, v, ktx, vx, bias, batch, ctx_len):
    assert ctx_len == ATT_TQ
    n = q.shape[1]
    nblk = n // batch // ATT_TQ
    cur = lambda b, i: b * nblk + i
    prev = lambda b, i: b * nblk + jnp.maximum(i - 1, 0)
    nxt = lambda b, i: b * nblk + jnp.minimum(i + 1, nblk - 1)
    variant = lambda b, i: (jnp.where(i == 0, 0, jnp.where(i == nblk - 1, 2, 1)), 0, 0, 0)
    rows = lambda f: pl.BlockSpec((HEAD_PAIRS, ATT_TQ, V7X_LANES), lambda b, i: (0, f(b, i), 0))
    cols = lambda f: pl.BlockSpec((NA_WIDTH, ATT_TQ), lambda b, i: (0, f(b, i)))
    vmem = 2 * bias[0].size * 4 + 24 * ATT_TQ * NA_WIDTH * 2 + 16 * ATT_TQ * ATT_TQ * 4 + 16 * ATT_TQ * ATT_TQ * 4
    return pl.pallas_call(
        _na_kernel,
        out_shape=jax.ShapeDtypeStruct((HEAD_PAIRS, n, V7X_LANES), BF16),
        grid=(batch, nblk),
        in_specs=[rows(cur), cols(prev), cols(cur), cols(nxt), rows(prev), rows(cur), rows(nxt),
                  cols(lambda b, i: b), rows(lambda b, i: b),
                  pl.BlockSpec((None,) + bias.shape[1:], variant)],
        out_specs=rows(cur),
        compiler_params=pltpu.CompilerParams(
            dimension_semantics=("arbitrary", "arbitrary"), vmem_limit_bytes=_vmem_limit(vmem)),
        name="neighbourhood_attention",
    )(q, kt, kt, kt, v, v, v, ktx, vx, bias)


def _ctx_attn_kernel(q_ref, kt_ref, v_ref, o_ref):
    for p in range(HEAD_PAIRS):
        outs = []
        for hh in range(2):
            lanes = slice(hh * HEAD_DIM, (hh + 1) * HEAD_DIM)
            s = _dot(q_ref[p][:, lanes], kt_ref[pl.ds(p * V7X_LANES + hh * HEAD_DIM, HEAD_DIM), :])
            outs.append(_softmax_pv([s], [v_ref[p][:, lanes]]))
        o_ref[p] = jnp.concatenate(outs, axis=1).astype(o_ref.dtype)


def _context_attention(q, kt, v, batch):
    n = q.shape[1]
    spec = pl.BlockSpec((HEAD_PAIRS, n // batch, V7X_LANES), lambda b: (0, b, 0))
    return pl.pallas_call(
        _ctx_attn_kernel,
        out_shape=jax.ShapeDtypeStruct((HEAD_PAIRS, n, V7X_LANES), BF16),
        grid=(batch,),
        in_specs=[spec, pl.BlockSpec((NA_WIDTH, n // batch), lambda b: (0, b)), spec],
        out_specs=spec,
        compiler_params=pltpu.CompilerParams(dimension_semantics=("arbitrary",)),
        name="context_attention",
    )(q, kt, v)


def _rope_tables(seq):
    t = np.arange(seq)
    inv = ROPE_THETA ** (-np.arange(ROPE_PAIRS, dtype=np.float64) / ROPE_PAIRS)
    ang_r = (t // GRID_W)[:, None] * inv
    ang_c = (t % GRID_W)[:, None] * inv
    ang = np.concatenate([ang_r, ang_r, ang_c, ang_c], axis=1)
    first = np.tile(np.arange(HEAD_DIM) % (2 * ROPE_PAIRS) < ROPE_PAIRS, 2)
    cos = np.tile(np.cos(ang), (1, 2))
    sin = np.tile(np.sin(ang), (1, 2))
    sa = np.where(first, -sin, 0.0)
    sb = np.where(first, 0.0, sin)
    return tuple(jnp.asarray(a, F32) for a in (cos, sa, sb))


def _identity_rope_tables(n):
    ones = jnp.ones((n, V7X_LANES), F32)
    zeros = jnp.zeros((n, V7X_LANES), F32)
    return ones, zeros, zeros


def _pool_tables(tm, tb, length):
    assert tb % length == 0 and tm % tb == 0
    t = np.arange(tm)
    pos = t % length
    base = t - pos
    bands, icnts = [], []
    for w in POOL_WINDOWS:
        lo = np.clip(pos - w // 2, 0, length)
        hi = np.clip(pos - w // 2 + w, 0, length)
        s = t[None, :tb]
        bands.append((s >= (base + lo)[:tb, None]) & (s < (base + hi)[:tb, None]))
        icnts.append(np.repeat((1.0 / (hi - lo))[:, None], POOL_CH, axis=1))
    band = jnp.asarray(np.stack(bands), BF16)
    icnt = jnp.asarray(np.concatenate(icnts, axis=1), F32)
    return band, icnt


def _attention_bias(rpb, rows):
    heads = rpb.shape[0]
    kh = min(NA_KH, rows)
    assert kh == NA_KH and ATT_ROWS == NA_KH // 2 and rows >= 3 * ATT_ROWS
    by_row = jnp.stack([rpb[:, ATT_ROWS - 1 - i:ATT_ROWS - 1 - i + 3 * ATT_ROWS] for i in range(ATT_ROWS)], axis=1)
    pad = GRID_W - NA_KW
    padded = jnp.pad(by_row, ((0, 0), (0, 0), (0, 0), (pad, pad)))
    toep = jnp.stack([padded[..., GRID_W - 1 - j:2 * GRID_W - 1 - j] for j in range(GRID_W)], axis=3)
    base = jnp.transpose(toep, (0, 1, 3, 2, 4)).reshape(heads, ATT_TQ, 3 * ATT_TQ)

    i = np.arange(ATT_TQ) // GRID_W
    j = np.arange(ATT_TQ) % GRID_W
    m = np.arange(3 * ATT_TQ) // GRID_W
    c = np.arange(3 * ATT_TQ) % GRID_W
    col_start = np.clip(j - NA_KW // 2, 0, GRID_W - NA_KW)
    col_ok = (c[None, :] >= col_start[:, None]) & (c[None, :] < col_start[:, None] + NA_KW)
    masks = []
    for r0 in (0, ATT_ROWS, rows - ATT_ROWS):
        r = r0 + i
        kr = r0 - ATT_ROWS + m
        row_start = np.clip(r - kh // 2, 0, rows - kh)
        row_ok = (kr[None, :] >= row_start[:, None]) & (kr[None, :] < row_start[:, None] + kh)
        masks.append(row_ok & col_ok)
    return jnp.where(np.stack(masks)[:, None], base[None] * LOG2_E, NEG_INF).astype(F32)


def kernel(x, c, ctx, c_ctx, w_mod, b_mod, norm_g, w_ffn_gate_up, w_ffn_down, w_in, w_out, na_rpb, w_pool, pool_scale):
    batch, seq, d = x.shape
    ctx_len = ctx.shape[1]
    depth = w_mod.shape[0]
    rows = seq // GRID_W
    tm_x, tm_c = 512, ctx_len

    cv = jnp.zeros((V7X_SUBLANES, d), F32).at[:batch].set(c).at[batch].set(c_ctx)
    mods = _mod_vectors(cv, w_mod, b_mod)[:, :batch + 1].reshape(depth, batch + 1, N_MOD, 1, d)

    wgu, wdn = w_ffn_gate_up.astype(BF16), w_ffn_down.astype(BF16)
    w_in_b, w_out_b, w_pool_b = w_in.astype(BF16), w_out.astype(BF16), w_pool.astype(BF16)
    norm = norm_g.reshape(depth, 6, 1, d)

    rope_x = _rope_tables(seq)
    rope_c = _identity_rope_tables(tm_c)
    band_x, icnt_x = _pool_tables(tm_x, 2 * GRID_W, GRID_W)
    band_c, icnt_c = _pool_tables(tm_c, ctx_len, ctx_len)

    xt = x.reshape(batch * seq, d)
    ct = ctx.reshape(batch * ctx_len, d)
    for l in range(depth):
        last = l == depth - 1
        mx, mc = mods[l, :batch], mods[l, batch:]
        g = norm[l]
        ps = pool_scale[l].reshape(1, POOL_WIDTH)
        first_ffn = lambda t, m, tm: _ffn(t, m[:, 0:3], g[0:2], wgu, wdn, (l, 0), tm)
        mix_ffn = lambda t, m, tm, att, pool: _ffn(t, m[:, 5:9], g[3:6], wgu, wdn, (l, 1), tm,
                                                   mix=(att, pool, w_out_b[l]))
        xt = first_ffn(xt, mx, tm_x)
        ct = first_ffn(ct, mc, tm_c)
        qx, ktx, vx, px = _inproj(xt, mx[:, 3:5], g[2:3], w_in_b[l], rope_x, band_x, icnt_x, w_pool_b[l], ps, tm_x)
        qc, ktc, vc, pc = _inproj(ct, mc[:, 3:5], g[2:3], w_in_b[l], rope_c, band_c, icnt_c, w_pool_b[l], ps, tm_c)
        bias = _attention_bias(na_rpb[l], rows)
        ax = _neighbourhood_attention(qx, ktx, vx, ktc, vc, bias, batch, ctx_len)
        xt = mix_ffn(xt, mx, tm_x, ax, px)
        if not last:
            ac = _context_attention(qc, ktc, vc, batch)
            ct = mix_ffn(ct, mc, tm_c, ac, pc)
    return xt.reshape(batch, seq, d)
```

```python
import functools

import numpy as np
import jax
import jax.numpy as jnp
from jax import lax
from jax.experimental import pallas as pl
from jax.experimental.pallas import tpu as pltpu

GRID_W = 64
N_MOD = 9
NA_HEADS = 8
HEAD_DIM = 64
NA_WIDTH = NA_HEADS * HEAD_DIM
NA_KH = 8
NA_KW = 16
POOL_GROUPS = 4
POOL_CH = 128
POOL_WIDTH = POOL_GROUPS * POOL_CH
POOL_WINDOWS = (2, 4, 8, 16)
ROPE_THETA = 10000.0
ROPE_PAIRS = HEAD_DIM // 4
RMS_EPS = 1e-6
NEG_INF = -1e30
LOG2_E = 1.4426950408889634

V7X_LANES = 128
V7X_SUBLANES = 8
V7X_VMEM_BYTES = 64 * 1024 * 1024

FF_CHUNK = 256
ATT_ROWS = 4
ATT_TQ = ATT_ROWS * GRID_W

F32 = jnp.float32
BF16 = jnp.bfloat16


def _vmem_limit(nbytes):
    return int(min(nbytes + (8 << 20), V7X_VMEM_BYTES - (6 << 20)))


def _resident(block_shape, lead=()):
    index = tuple(lead) + (0,) * len(block_shape)
    return pl.BlockSpec((None,) * len(lead) + tuple(block_shape), lambda *_: index, pipeline_mode=pl.Buffered(1))


def _rms(x, g):
    return x * lax.rsqrt(jnp.mean(x * x, axis=-1, keepdims=True) + RMS_EPS) * g


def _dot(a, b):
    return jnp.dot(a, b, preferred_element_type=F32)


def _split_bf16(x):
    hi = x.astype(BF16)
    lo = (x - hi.astype(F32)).astype(BF16)
    return hi, lo


def _mod_kernel(cv_ref, w_ref, b_ref, o_ref):
    s = cv_ref[...]
    s = s * jax.nn.sigmoid(s)
    s_hi, s_lo = _split_bf16(s)
    w_hi, w_lo = _split_bf16(w_ref[...])
    n = s.shape[0]
    r = _dot(jnp.concatenate([s_hi, s_lo], axis=0), w_hi)
    o_ref[...] = r[:n] + r[n:] + _dot(s_hi, w_lo) + b_ref[...]


def _mod_vectors(cv, w_mod, b_mod):
    depth, d, n = w_mod.shape
    tn = 1024
    return pl.pallas_call(
        _mod_kernel,
        out_shape=jax.ShapeDtypeStruct((depth, cv.shape[0], n), F32),
        grid=(depth, n // tn),
        in_specs=[
            pl.BlockSpec(cv.shape, lambda l, j: (0, 0)),
            pl.BlockSpec((None, d, tn), lambda l, j: (l, 0, j)),
            pl.BlockSpec((None, 1, tn), lambda l, j: (l, 0, j)),
        ],
        out_specs=pl.BlockSpec((None, cv.shape[0], tn), lambda l, j: (l, 0, j)),
        compiler_params=pltpu.CompilerParams(
            dimension_semantics=("arbitrary", "arbitrary"),
            vmem_limit_bytes=_vmem_limit(5 * d * tn * 4)),
        name="mod_vectors",
    )(cv, w_mod, b_mod.reshape(depth, 1, n))


class _Cond:
    def __init__(self, mods, norm, layer, row0, n_rows):
        self.mods, self.norm, self.layer, self.row0, self.n_rows = mods, norm, layer, row0, n_rows

    def specs(self, n_tiles):
        per_row = n_tiles // self.n_rows
        layer, row0 = self.layer, self.row0
        return [pl.BlockSpec((None, None) + self.mods.shape[2:], lambda i: (layer, row0 + i // per_row, 0, 0, 0)),
                pl.BlockSpec((None,) + self.norm.shape[1:], lambda i: (layer, 0, 0, 0))]


def _ffn_kernel(*refs, mixed, m0, g0):
    if mixed:
        x_ref, a_ref, p_ref, wo_ref, mod_ref, g_ref, wgu_ref, wd_ref, o_ref, h_scr, acc_scr = refs
        att = jnp.concatenate([a_ref[c] for c in range(a_ref.shape[0])], axis=1)
        y = _dot(att, wo_ref[pl.ds(0, NA_WIDTH), :]) + _dot(p_ref[...], wo_ref[pl.ds(NA_WIDTH, POOL_WIDTH), :])
        o_ref[...] = x_ref[...] + mod_ref[m0] * _rms(y, g_ref[g0])
        x_ref, m0, g0 = o_ref, m0 + 1, g0 + 1
    else:
        x_ref, mod_ref, g_ref, wgu_ref, wd_ref, o_ref, h_scr, acc_scr = refs
    mod_ref, g_ref = mod_ref.at[pl.ds(m0, 3)], g_ref.at[pl.ds(g0, 2)]
    h = _rms(x_ref[...], g_ref[0]) * (1.0 + mod_ref[1]) + mod_ref[0]
    h_scr[...] = h.astype(BF16)
    d_ff = wd_ref.shape[0]
    for j in range(d_ff // FF_CHUNK):
        hb = h_scr[...]
        gate = _dot(hb, wgu_ref[:, pl.ds(j * FF_CHUNK, FF_CHUNK)])
        up = _dot(hb, wgu_ref[:, pl.ds(d_ff + j * FF_CHUNK, FF_CHUNK)])
        act = (gate * jax.nn.sigmoid(gate) * up).astype(BF16)
        part = _dot(act, wd_ref[pl.ds(j * FF_CHUNK, FF_CHUNK), :])
        if j == 0:
            acc_scr[...] = part
        else:
            acc_scr[...] += part
    y = _rms(acc_scr[...], g_ref[1])
    o_ref[...] = x_ref[...] + 0.5 * mod_ref[2] * y


def _ffn(xt, cond, wgu, wd, half, tm, mix=None):
    n, d = xt.shape
    nb = n // tm
    which = (cond.layer, half)
    d_ff = wd.shape[2]
    assert d_ff % FF_CHUNK == 0 and wgu.shape[2:] == (d, 2 * d_ff)
    tok = pl.BlockSpec((tm, d), lambda i: (i, 0))
    vmem = 4 * tm * d * 4 + tm * d * 2 + tm * d * 4 + 3 * d * d_ff * 2 + 4 * tm * FF_CHUNK * 4
    operands, specs = [xt], [tok]
    if mix is not None:
        att, pool, w_out = mix
        operands += [att, pool, w_out]
        specs += [pl.BlockSpec((att.shape[0], tm, att.shape[2]), lambda i: (0, i, 0)),
                  pl.BlockSpec((tm, pool.shape[1]), lambda i: (i, 0)), _resident(w_out.shape[1:], (cond.layer,))]
        vmem += 4 * tm * pool.shape[1] * 2 + 2 * d * d * 2 + 2 * tm * d * 4
    operands += [cond.mods, cond.norm, wgu, wd]
    specs += cond.specs(nb) + [_resident(wgu.shape[2:], which), _resident(wd.shape[2:], which)]
    m0, g0 = (5, 3) if mix is not None else (3 * 2 * half, 4 * half)
    return pl.pallas_call(
        functools.partial(_ffn_kernel, mixed=mix is not None, m0=m0, g0=g0),
        out_shape=jax.ShapeDtypeStruct((n, d), F32),
        grid=(nb,),
        in_specs=specs,
        out_specs=tok,
        scratch_shapes=[pltpu.VMEM((tm, d), BF16), pltpu.VMEM((tm, d), F32)],
        compiler_params=pltpu.CompilerParams(
            dimension_semantics=("arbitrary",), vmem_limit_bytes=_vmem_limit(vmem)),
        name="mix_ffn" if mix is not None else "sandwich_ffn",
    )(*operands)


def _inproj_kernel(x_ref, mod_ref, g_ref, w_ref, cos_ref, sa_ref, sb_ref, band_ref, icnt_ref,
                   wp_ref, ps_ref, q_ref, kt_ref, v_ref, p_ref):
    h = (_rms(x_ref[...], g_ref[2]) * (1.0 + mod_ref[4]) + mod_ref[3]).astype(BF16)
    cos, sa, sb = cos_ref[...], sa_ref[...], sb_ref[...]

    def rope(t):
        up = pltpu.roll(t, V7X_LANES - ROPE_PAIRS, 1)
        down = pltpu.roll(t, ROPE_PAIRS, 1)
        return t * cos + up * sa + down * sb

    scale = HEAD_DIM ** -0.5 * LOG2_E
    qk =_dot(h, w_ref[:, pl.ds(0, 2 * NA_WIDTH)])
    for c in range(NA_WIDTH // V7X_LANES):
        lanes = pl.ds(c * V7X_LANES, V7X_LANES)
        q_ref[c] = (rope(qk[:, c * V7X_LANES:(c + 1) * V7X_LANES]) * scale).astype(BF16)
        k_rot = rope(qk[:, NA_WIDTH + c * V7X_LANES:NA_WIDTH + (c + 1) * V7X_LANES])
        kt_ref[lanes, :] = k_rot.T.astype(BF16)
    vu = _dot(h, w_ref[:, pl.ds(2 * NA_WIDTH, NA_WIDTH + POOL_WIDTH)])
    for c in range(NA_WIDTH // V7X_LANES):
        v_ref[c] = vu[:, c * V7X_LANES:(c + 1) * V7X_LANES].astype(BF16)
    tm, tb = x_ref.shape[0], band_ref.shape[1]
    for g in range(POOL_GROUPS):
        lanes = pl.ds(g * POOL_CH, POOL_CH)
        u = vu[:, NA_WIDTH + g * POOL_CH:NA_WIDTH + (g + 1) * POOL_CH]
        u2 = jnp.concatenate(_split_bf16(u), axis=1)
        band = band_ref[g]
        win2 = jnp.concatenate([_dot(band, u2[s * tb:(s + 1) * tb]) for s in range(tm // tb)], axis=0)
        win = win2[:, :POOL_CH] + win2[:, POOL_CH:]
        dlt = (win * icnt_ref[:, lanes] - u).astype(BF16)
        p_ref[:, lanes] = (_dot(dlt, wp_ref[g]) * ps_ref[:, lanes]).astype(BF16)


def _inproj(xt, cond, w_in, rope_tabs, band, icnt, w_pool, pool_scale, tm):
    n, d = xt.shape
    nb = n // tm
    layer = (cond.layer,)
    tab_blocks = rope_tabs[0].shape[0] // tm
    tab_spec = pl.BlockSpec((tm, V7X_LANES), lambda i: (i % tab_blocks, 0))
    out = jax.ShapeDtypeStruct((n, NA_WIDTH), BF16)
    out_spec = pl.BlockSpec((tm, NA_WIDTH), lambda i: (i, 0))
    vmem = (2 * tm * d * 4 + tm * d * 2 + w_in[0].size * 2 + 6 * tm * V7X_LANES * 4 + band.size * 2
            + icnt.size * 4 + 8 * tm * NA_WIDTH * 2 + 6 * tm * NA_WIDTH * 4)
    out_t = jax.ShapeDtypeStruct((NA_WIDTH, n), BF16)
    out_t_spec = pl.BlockSpec((NA_WIDTH, tm), lambda i: (0, i))
    pairs = NA_WIDTH // V7X_LANES
    out_p = jax.ShapeDtypeStruct((pairs, n, V7X_LANES), BF16)
    out_p_spec = pl.BlockSpec((pairs, tm, V7X_LANES), lambda i: (0, i, 0))
    return pl.pallas_call(
        _inproj_kernel,
        out_shape=(out_p, out_t, out_p, out),
        grid=(nb,),
        in_specs=[pl.BlockSpec((tm, d), lambda i: (i, 0))] + cond.specs(nb) + [
            _resident(w_in.shape[1:], layer),
            tab_spec, tab_spec, tab_spec,
            _resident(band.shape), _resident(icnt.shape), _resident(w_pool.shape[1:], layer),
            _resident(pool_scale.shape[1:], layer),
        ],
        out_specs=(out_p_spec, out_t_spec, out_p_spec, out_spec),
        compiler_params=pltpu.CompilerParams(
            dimension_semantics=("arbitrary",), vmem_limit_bytes=_vmem_limit(vmem)),
        name="in_projection",
    )(xt, cond.mods, cond.norm, w_in, *rope_tabs, band, icnt, w_pool, pool_scale)


def _softmax_pv(scores, values):
    assert len({s.shape for s in scores}) == 1
    m = jnp.max(functools.reduce(jnp.maximum, scores), axis=-1, keepdims=True)
    probs = [jnp.exp2(s - m) for s in scores]
    denom = jnp.sum(functools.reduce(lambda a, b: a + b, probs), axis=-1, keepdims=True)
    o = functools.reduce(lambda a, b: a + b, [_dot(p.astype(BF16), v) for p, v in zip(probs, values)])
    return o / denom


HEAD_PAIRS = NA_WIDTH // V7X_LANES


def _na_kernel(q_ref, kp_ref, kc_ref, kn_ref, vp_ref, vc_ref, vn_ref, kx_ref, vx_ref, bias_ref, mask_ref, o_ref):
    kt_refs = (kp_ref, kc_ref, kn_ref, kx_ref)
    v_refs = (vp_ref, vc_ref, vn_ref, vx_ref)
    n_loc = 3 * ATT_TQ
    mask = mask_ref[...]
    for p in range(HEAD_PAIRS):
        outs = []
        for hh in range(2):
            h = 2 * p + hh
            lanes = slice(hh * HEAD_DIM, (hh + 1) * HEAD_DIM)
            feat = pl.ds(h * HEAD_DIM, HEAD_DIM)
            kt = jnp.concatenate([r[feat, :] for r in kt_refs], axis=1)
            val = jnp.concatenate([r[p][:, lanes] for r in v_refs], axis=0)
            s = _dot(q_ref[p][:, lanes], kt)
            s = jnp.concatenate([s[:, :n_loc] + bias_ref[h] + mask, s[:, n_loc:]], axis=1)
            m = jnp.max(s, axis=-1, keepdims=True)
            e = jnp.exp2(s - m)
            denom = jnp.sum(e, axis=-1, keepdims=True)
            outs.append(_dot(e.astype(BF16), val) / denom)
        o_ref[p] = jnp.concatenate(outs, axis=1).astype(o_ref.dtype)


def _neighbourhood_attention(q, kt, v, ktx, vx, bias, masks, layer, batch, ctx_len):
    assert ctx_len == ATT_TQ
    n = q.shape[1]
    nblk = n // batch // ATT_TQ
    cur = lambda b, i: b * nblk + i
    prev = lambda b, i: b * nblk + jnp.maximum(i - 1, 0)
    nxt = lambda b, i: b * nblk + jnp.minimum(i + 1, nblk - 1)
    variant = lambda b, i: (jnp.where(i == 0, 0, jnp.where(i == nblk - 1, 2, 1)), 0, 0)
    rows = lambda f: pl.BlockSpec((HEAD_PAIRS, ATT_TQ, V7X_LANES), lambda b, i: (0, f(b, i), 0))
    cols = lambda f: pl.BlockSpec((NA_WIDTH, ATT_TQ), lambda b, i: (0, f(b, i)))
    vmem = (bias[0].size + 2 * masks[0].size) * 4 + 24 * ATT_TQ * NA_WIDTH * 2 + 32 * ATT_TQ * ATT_TQ * 4
    return pl.pallas_call(
        _na_kernel,
        out_shape=jax.ShapeDtypeStruct((HEAD_PAIRS, n, V7X_LANES), BF16),
        grid=(batch, nblk),
        in_specs=[rows(cur), cols(prev), cols(cur), cols(nxt), rows(prev), rows(cur), rows(nxt),
                  cols(lambda b, i: b), rows(lambda b, i: b),
                  _resident(bias.shape[1:], (layer,)),
                  pl.BlockSpec((None,) + masks.shape[1:], variant)],
        out_specs=rows(cur),
        compiler_params=pltpu.CompilerParams(
            dimension_semantics=("arbitrary", "arbitrary"), vmem_limit_bytes=_vmem_limit(vmem)),
        name="neighbourhood_attention",
    )(q, kt, kt, kt, v, v, v, ktx, vx, bias, masks)


def _ctx_attn_kernel(q_ref, kt_ref, v_ref, o_ref):
    for p in range(HEAD_PAIRS):
        outs = []
        for hh in range(2):
            lanes = slice(hh * HEAD_DIM, (hh + 1) * HEAD_DIM)
            s = _dot(q_ref[p][:, lanes], kt_ref[pl.ds(p * V7X_LANES + hh * HEAD_DIM, HEAD_DIM), :])
            outs.append(_softmax_pv([s], [v_ref[p][:, lanes]]))
        o_ref[p] = jnp.concatenate(outs, axis=1).astype(o_ref.dtype)


def _context_attention(q, kt, v, batch):
    n = q.shape[1]
    spec = pl.BlockSpec((HEAD_PAIRS, n // batch, V7X_LANES), lambda b: (0, b, 0))
    return pl.pallas_call(
        _ctx_attn_kernel,
        out_shape=jax.ShapeDtypeStruct((HEAD_PAIRS, n, V7X_LANES), BF16),
        grid=(batch,),
        in_specs=[spec, pl.BlockSpec((NA_WIDTH, n // batch), lambda b: (0, b)), spec],
        out_specs=spec,
        compiler_params=pltpu.CompilerParams(dimension_semantics=("arbitrary",)),
        name="context_attention",
    )(q, kt, v)


def _rope_tables(seq):
    t = np.arange(seq)
    inv = ROPE_THETA ** (-np.arange(ROPE_PAIRS, dtype=np.float64) / ROPE_PAIRS)
    ang_r = (t // GRID_W)[:, None] * inv
    ang_c = (t % GRID_W)[:, None] * inv
    ang = np.concatenate([ang_r, ang_r, ang_c, ang_c], axis=1)
    first = np.tile(np.arange(HEAD_DIM) % (2 * ROPE_PAIRS) < ROPE_PAIRS, 2)
    cos = np.tile(np.cos(ang), (1, 2))
    sin = np.tile(np.sin(ang), (1, 2))
    sa = np.where(first, -sin, 0.0)
    sb = np.where(first, 0.0, sin)
    return tuple(jnp.asarray(a, F32) for a in (cos, sa, sb))


def _identity_rope_tables(n):
    ones = jnp.ones((n, V7X_LANES), F32)
    zeros = jnp.zeros((n, V7X_LANES), F32)
    return ones, zeros, zeros


def _pool_tables(tm, tb, length):
    assert tb % length == 0 and tm % tb == 0
    t = np.arange(tm)
    pos = t % length
    base = t - pos
    bands, icnts = [], []
    for w in POOL_WINDOWS:
        lo = np.clip(pos - w // 2, 0, length)
        hi = np.clip(pos - w // 2 + w, 0, length)
        s = t[None, :tb]
        bands.append((s >= (base + lo)[:tb, None]) & (s < (base + hi)[:tb, None]))
        icnts.append(np.repeat((1.0 / (hi - lo))[:, None], POOL_CH, axis=1))
    band = jnp.asarray(np.stack(bands), BF16)
    icnt = jnp.asarray(np.concatenate(icnts, axis=1), F32)
    return band, icnt


def _attention_bias(rpb, rows):
    kh = min(NA_KH, rows)
    assert kh == NA_KH and ATT_ROWS == NA_KH // 2 and rows >= 3 * ATT_ROWS
    rpb = rpb * LOG2_E
    by_row = jnp.stack([rpb[:, :, ATT_ROWS - 1 - i:ATT_ROWS - 1 - i + 3 * ATT_ROWS] for i in range(ATT_ROWS)], axis=2)
    pad = GRID_W - NA_KW
    padded = jnp.pad(by_row, ((0, 0),) * 4 + ((pad, pad),))
    toep = jnp.stack([padded[..., GRID_W - 1 - j:2 * GRID_W - 1 - j] for j in range(GRID_W)], axis=4)
    base = jnp.transpose(toep, (0, 1, 2, 4, 3, 5)).reshape(rpb.shape[:2] + (ATT_TQ, 3 * ATT_TQ))

    i = np.arange(ATT_TQ) // GRID_W
    j = np.arange(ATT_TQ) % GRID_W
    m = np.arange(3 * ATT_TQ) // GRID_W
    c = np.arange(3 * ATT_TQ) % GRID_W
    col_start = np.clip(j - NA_KW // 2, 0, GRID_W - NA_KW)
    col_ok = (c[None, :] >= col_start[:, None]) & (c[None, :] < col_start[:, None] + NA_KW)
    masks = []
    for r0 in (0, ATT_ROWS, rows - ATT_ROWS):
        r = r0 + i
        kr = r0 - ATT_ROWS + m
        row_start = np.clip(r - kh // 2, 0, rows - kh)
        row_ok = (kr[None, :] >= row_start[:, None]) & (kr[None, :] < row_start[:, None] + kh)
        masks.append(np.where(row_ok & col_ok, 0.0, NEG_INF))
    return base.astype(F32), jnp.asarray(np.stack(masks), F32)


def kernel(x, c, ctx, c_ctx, w_mod, b_mod, norm_g, w_ffn_gate_up, w_ffn_down, w_in, w_out, na_rpb, w_pool, pool_scale):
    batch, seq, d = x.shape
    ctx_len = ctx.shape[1]
    depth = w_mod.shape[0]
    rows = seq // GRID_W
    tm_x, tm_c = 512, ctx_len

    cv = jnp.concatenate([c, c_ctx[None], jnp.zeros((V7X_SUBLANES - batch - 1, d), F32)])
    mods = _mod_vectors(cv, w_mod, b_mod).reshape(depth, V7X_SUBLANES, N_MOD, 1, d)

    wgu, wdn = w_ffn_gate_up.astype(BF16), w_ffn_down.astype(BF16)
    w_in_b, w_out_b, w_pool_b = w_in.astype(BF16), w_out.astype(BF16), w_pool.astype(BF16)
    norm = norm_g.reshape(depth, norm_g.shape[1], 1, d)
    ps = pool_scale.reshape(depth, 1, POOL_WIDTH)
    bias, masks = _attention_bias(na_rpb, rows)

    rope_x = _rope_tables(seq)
    rope_c = _identity_rope_tables(tm_c)
    band_x, icnt_x = _pool_tables(tm_x, 2 * GRID_W, GRID_W)
    band_c, icnt_c = _pool_tables(tm_c, ctx_len, ctx_len)

    xt = x.reshape(batch * seq, d)
    ct = ctx.reshape(batch * ctx_len, d)
    for l in range(depth):
        last = l == depth - 1
        cx = _Cond(mods, norm, l, 0, batch)
        cc = _Cond(mods, norm, l, batch, 1)
        xt = _ffn(xt, cx, wgu, wdn, 0, tm_x)
        ct = _ffn(ct, cc, wgu, wdn, 0, tm_c)
        qx, ktx, vx, px = _inproj(xt, cx, w_in_b, rope_x, band_x, icnt_x, w_pool_b, ps, tm_x)
        qc, ktc, vc, pc = _inproj(ct, cc, w_in_b, rope_c, band_c, icnt_c, w_pool_b, ps, tm_c)
        ax = _neighbourhood_attention(qx, ktx, vx, ktc, vc, bias, masks, l, batch, ctx_len)
        xt = _ffn(xt, cx, wgu, wdn, 1, tm_x, mix=(ax, px, w_out_b))
        if not last:
            ac = _context_attention(qc, ktc, vc, batch)
            ct = _ffn(ct, cc, wgu, wdn, 1, tm_c, mix=(ac, pc, w_out_b))
    return xt.reshape(batch, seq, d)
```

```python
import functools

import numpy as np
import jax
import jax.numpy as jnp
from jax import lax
from jax.experimental import pallas as pl
from jax.experimental.pallas import tpu as pltpu

GRID_W = 64
N_MOD = 9
NA_HEADS = 8
HEAD_DIM = 64
NA_WIDTH = NA_HEADS * HEAD_DIM
NA_KH = 8
NA_KW = 16
POOL_GROUPS = 4
POOL_CH = 128
POOL_WIDTH = POOL_GROUPS * POOL_CH
POOL_WINDOWS = (2, 4, 8, 16)
ROPE_THETA = 10000.0
ROPE_PAIRS = HEAD_DIM // 4
RMS_EPS = 1e-6
NEG_INF = -1e30
LOG2_E = 1.4426950408889634

V7X_LANES = 128
V7X_SUBLANES = 8
V7X_VMEM_BYTES = 64 * 1024 * 1024

FF_CHUNK = 256
ATT_ROWS = 4
ATT_TQ = ATT_ROWS * GRID_W

F32 = jnp.float32
BF16 = jnp.bfloat16


def _vmem_limit(nbytes):
    return int(min(nbytes + (8 << 20), V7X_VMEM_BYTES - (6 << 20)))


def _resident(block_shape, lead=()):
    index = tuple(lead) + (0,) * len(block_shape)
    return pl.BlockSpec((None,) * len(lead) + tuple(block_shape), lambda *_: index, pipeline_mode=pl.Buffered(1))


def _rms(x, g):
    return x * lax.rsqrt(jnp.mean(x * x, axis=-1, keepdims=True) + RMS_EPS) * g


def _dot(a, b):
    return jnp.dot(a, b, preferred_element_type=F32)


def _split_bf16(x):
    hi = x.astype(BF16)
    lo = (x - hi.astype(F32)).astype(BF16)
    return hi, lo


def _mod_kernel(cv_ref, w_ref, b_ref, o_ref):
    s = cv_ref[...]
    s = s * jax.nn.sigmoid(s)
    s_hi, s_lo = _split_bf16(s)
    w_hi, w_lo = _split_bf16(w_ref[...])
    n = s.shape[0]
    r = _dot(jnp.concatenate([s_hi, s_lo], axis=0), w_hi)
    o_ref[...] = r[:n] + r[n:] + _dot(s_hi, w_lo) + b_ref[...]


def _mod_vectors(cv, w_mod, b_mod):
    depth, d, n = w_mod.shape
    tn = 1024
    return pl.pallas_call(
        _mod_kernel,
        out_shape=jax.ShapeDtypeStruct((depth, cv.shape[0], n), F32),
        grid=(depth, n // tn),
        in_specs=[
            pl.BlockSpec(cv.shape, lambda l, j: (0, 0)),
            pl.BlockSpec((None, d, tn), lambda l, j: (l, 0, j)),
            pl.BlockSpec((None, 1, tn), lambda l, j: (l, 0, j)),
        ],
        out_specs=pl.BlockSpec((None, cv.shape[0], tn), lambda l, j: (l, 0, j)),
        compiler_params=pltpu.CompilerParams(
            dimension_semantics=("arbitrary", "arbitrary"),
            vmem_limit_bytes=_vmem_limit(5 * d * tn * 4)),
        name="mod_vectors",
    )(cv, w_mod, b_mod.reshape(depth, 1, n))


class _Cond:
    def __init__(self, mods, norm, layer, row0, n_rows):
        self.mods, self.norm, self.layer, self.row0, self.n_rows = mods, norm, layer, row0, n_rows

    def specs(self, n_tiles):
        per_row = n_tiles // self.n_rows
        layer, row0 = self.layer, self.row0
        return [pl.BlockSpec((None, None) + self.mods.shape[2:], lambda i: (layer, row0 + i // per_row, 0, 0, 0)),
                pl.BlockSpec((None,) + self.norm.shape[1:], lambda i: (layer, 0, 0, 0))]


def _ffn_kernel(*refs, mixed, m0, g0):
    if mixed:
        x_ref, a_ref, p_ref, wo_ref, mod_ref, g_ref, wgu_ref, wd_ref, o_ref, h_scr, acc_scr = refs
        att = jnp.concatenate([a_ref[c] for c in range(a_ref.shape[0])], axis=1)
        y = _dot(att, wo_ref[pl.ds(0, NA_WIDTH), :]) + _dot(p_ref[...], wo_ref[pl.ds(NA_WIDTH, POOL_WIDTH), :])
        o_ref[...] = x_ref[...] + _rms(y, mod_ref[m0] * g_ref[g0])
        x_ref, m0, g0 = o_ref, m0 + 1, g0 + 1
    else:
        x_ref, mod_ref, g_ref, wgu_ref, wd_ref, o_ref, h_scr, acc_scr = refs
    mod_ref, g_ref = mod_ref.at[pl.ds(m0, 3)], g_ref.at[pl.ds(g0, 2)]
    h = _rms(x_ref[...], g_ref[0] * (1.0 + mod_ref[1])) + mod_ref[0]
    h_scr[...] = h.astype(BF16)
    d_ff = wd_ref.shape[0]
    for j in range(d_ff // FF_CHUNK):
        hb = h_scr[...]
        gate = _dot(hb, wgu_ref[:, pl.ds(j * FF_CHUNK, FF_CHUNK)])
        up = _dot(hb, wgu_ref[:, pl.ds(d_ff + j * FF_CHUNK, FF_CHUNK)])
        act = (gate * jax.nn.sigmoid(gate) * up).astype(BF16)
        part = _dot(act, wd_ref[pl.ds(j * FF_CHUNK, FF_CHUNK), :])
        if j == 0:
            acc_scr[...] = part
        else:
            acc_scr[...] += part
    o_ref[...] = x_ref[...] + _rms(acc_scr[...], 0.5 * mod_ref[2] * g_ref[1])


def _ffn(xt, cond, wgu, wd, half, tm, mix=None):
    n, d = xt.shape
    nb = n // tm
    which = (cond.layer, half)
    d_ff = wd.shape[2]
    assert d_ff % FF_CHUNK == 0 and wgu.shape[2:] == (d, 2 * d_ff)
    tok = pl.BlockSpec((tm, d), lambda i: (i, 0))
    vmem = 4 * tm * d * 4 + tm * d * 2 + tm * d * 4 + 3 * d * d_ff * 2 + 4 * tm * FF_CHUNK * 4
    operands, specs = [xt], [tok]
    if mix is not None:
        att, pool, w_out = mix
        operands += [att, pool, w_out]
        specs += [pl.BlockSpec((att.shape[0], tm, att.shape[2]), lambda i: (0, i, 0)),
                  pl.BlockSpec((tm, pool.shape[1]), lambda i: (i, 0)), _resident(w_out.shape[1:], (cond.layer,))]
        vmem += 4 * tm * pool.shape[1] * 2 + 2 * d * d * 2 + 2 * tm * d * 4
    operands += [cond.mods, cond.norm, wgu, wd]
    specs += cond.specs(nb) + [_resident(wgu.shape[2:], which), _resident(wd.shape[2:], which)]
    m0, g0 = (5, 3) if mix is not None else (3 * 2 * half, 4 * half)
    return pl.pallas_call(
        functools.partial(_ffn_kernel, mixed=mix is not None, m0=m0, g0=g0),
        out_shape=jax.ShapeDtypeStruct((n, d), F32),
        grid=(nb,),
        in_specs=specs,
        out_specs=tok,
        scratch_shapes=[pltpu.VMEM((tm, d), BF16), pltpu.VMEM((tm, d), F32)],
        compiler_params=pltpu.CompilerParams(
            dimension_semantics=("arbitrary",), vmem_limit_bytes=_vmem_limit(vmem)),
        name="mix_ffn" if mix is not None else "sandwich_ffn",
    )(*operands)


def _inproj_kernel(x_ref, mod_ref, g_ref, w_ref, cos_ref, sa_ref, sb_ref, band_ref, icnt_ref,
                   wp_ref, ps_ref, q_ref, kt_ref, v_ref, p_ref):
    h = (_rms(x_ref[...], g_ref[2] * (1.0 + mod_ref[4])) + mod_ref[3]).astype(BF16)
    cos, sa, sb = cos_ref[...], sa_ref[...], sb_ref[...]

    def rope(t):
        up = pltpu.roll(t, V7X_LANES - ROPE_PAIRS, 1)
        down = pltpu.roll(t, ROPE_PAIRS, 1)
        return t * cos + up * sa + down * sb

    scale = HEAD_DIM ** -0.5 * LOG2_E
    qk =_dot(h, w_ref[:, pl.ds(0, 2 * NA_WIDTH)])
    for c in range(NA_WIDTH // V7X_LANES):
        lanes = pl.ds(c * V7X_LANES, V7X_LANES)
        q_ref[c] = (rope(qk[:, c * V7X_LANES:(c + 1) * V7X_LANES]) * scale).astype(BF16)
        k_rot = rope(qk[:, NA_WIDTH + c * V7X_LANES:NA_WIDTH + (c + 1) * V7X_LANES])
        kt_ref[lanes, :] = k_rot.T.astype(BF16)
    vu = _dot(h, w_ref[:, pl.ds(2 * NA_WIDTH, NA_WIDTH + POOL_WIDTH)])
    for c in range(NA_WIDTH // V7X_LANES):
        v_ref[c] = vu[:, c * V7X_LANES:(c + 1) * V7X_LANES].astype(BF16)
    tm, tb = x_ref.shape[0], band_ref.shape[1]
    for g in range(POOL_GROUPS):
        lanes = pl.ds(g * POOL_CH, POOL_CH)
        u = vu[:, NA_WIDTH + g * POOL_CH:NA_WIDTH + (g + 1) * POOL_CH]
        u2 = jnp.concatenate(_split_bf16(u), axis=1)
        band = band_ref[g]
        win2 = jnp.concatenate([_dot(band, u2[s * tb:(s + 1) * tb]) for s in range(tm // tb)], axis=0)
        win = win2[:, :POOL_CH] + win2[:, POOL_CH:]
        dlt = (win * icnt_ref[:, lanes] - u).astype(BF16)
        p_ref[:, lanes] = (_dot(dlt, wp_ref[g]) * ps_ref[:, lanes]).astype(BF16)


def _inproj(xt, cond, w_in, rope_tabs, band, icnt, w_pool, pool_scale, tm):
    n, d = xt.shape
    nb = n // tm
    layer = (cond.layer,)
    tab_blocks = rope_tabs[0].shape[0] // tm
    tab_spec = pl.BlockSpec((tm, V7X_LANES), lambda i: (i % tab_blocks, 0))
    out = jax.ShapeDtypeStruct((n, NA_WIDTH), BF16)
    out_spec = pl.BlockSpec((tm, NA_WIDTH), lambda i: (i, 0))
    vmem = (2 * tm * d * 4 + tm * d * 2 + w_in[0].size * 2 + 6 * tm * V7X_LANES * 4 + band.size * 2
            + icnt.size * 4 + 8 * tm * NA_WIDTH * 2 + 6 * tm * NA_WIDTH * 4)
    out_t = jax.ShapeDtypeStruct((NA_WIDTH, n), BF16)
    out_t_spec = pl.BlockSpec((NA_WIDTH, tm), lambda i: (0, i))
    pairs = NA_WIDTH // V7X_LANES
    out_p = jax.ShapeDtypeStruct((pairs, n, V7X_LANES), BF16)
    out_p_spec = pl.BlockSpec((pairs, tm, V7X_LANES), lambda i: (0, i, 0))
    return pl.pallas_call(
        _inproj_kernel,
        out_shape=(out_p, out_t, out_p, out),
        grid=(nb,),
        in_specs=[pl.BlockSpec((tm, d), lambda i: (i, 0))] + cond.specs(nb) + [
            _resident(w_in.shape[1:], layer),
            tab_spec, tab_spec, tab_spec,
            _resident(band.shape), _resident(icnt.shape), _resident(w_pool.shape[1:], layer),
            _resident(pool_scale.shape[1:], layer),
        ],
        out_specs=(out_p_spec, out_t_spec, out_p_spec, out_spec),
        compiler_params=pltpu.CompilerParams(
            dimension_semantics=("arbitrary",), vmem_limit_bytes=_vmem_limit(vmem)),
        name="in_projection",
    )(xt, cond.mods, cond.norm, w_in, *rope_tabs, band, icnt, w_pool, pool_scale)


def _softmax_pv(scores, values):
    assert len({s.shape for s in scores}) == 1
    m = jnp.max(functools.reduce(jnp.maximum, scores), axis=-1, keepdims=True)
    probs = [jnp.exp2(s - m) for s in scores]
    denom = jnp.sum(functools.reduce(lambda a, b: a + b, probs), axis=-1, keepdims=True)
    o = functools.reduce(lambda a, b: a + b, [_dot(p.astype(BF16), v) for p, v in zip(probs, values)])
    return o / denom


HEAD_PAIRS = NA_WIDTH // V7X_LANES


ATT_SUB = 2


def _na_kernel(q_ref, kp_ref, km_ref, kn_ref, vp_ref, vm_ref, vn_ref, kx_ref, vx_ref, bias_ref, mask_ref, o_ref,
               comb_scr):
    i, last = pl.program_id(1), pl.num_programs(1) - 1

    @pl.when((pl.program_id(0) == 0) & (i == 0))
    def _():
        for var in range(comb_scr.shape[0]):
            for h in range(NA_HEADS):
                comb_scr[var, h] = bias_ref[h] + mask_ref[var]

    n_loc = 3 * ATT_TQ
    for sub in range(ATT_SUB):
        variant = 1
        if sub == 0:
            variant = jnp.where(i == 0, 0, variant)
        if sub == ATT_SUB - 1:
            variant = jnp.where(i == last, 2, variant)
        qrows = slice(sub * ATT_TQ, (sub + 1) * ATT_TQ)
        mid = slice(max(sub - 1, 0) * ATT_TQ, min(sub + 2, ATT_SUB) * ATT_TQ)
        for p in range(HEAD_PAIRS):
            outs = []
            for hh in range(2):
                h = 2 * p + hh
                lanes = slice(hh * HEAD_DIM, (hh + 1) * HEAD_DIM)
                feat = pl.ds(h * HEAD_DIM, HEAD_DIM)
                kts = ([kp_ref[feat, :]] if sub == 0 else []) + [km_ref[feat, mid]]
                kts += ([kn_ref[feat, :]] if sub == ATT_SUB - 1 else []) + [kx_ref[feat, :]]
                vals = ([vp_ref[p][:, lanes]] if sub == 0 else []) + [vm_ref[p][mid, lanes]]
                vals += ([vn_ref[p][:, lanes]] if sub == ATT_SUB - 1 else []) + [vx_ref[p][:, lanes]]
                s = _dot(q_ref[p][qrows, lanes], jnp.concatenate(kts, axis=1))
                s = jnp.concatenate([s[:, :n_loc] + comb_scr[variant, h], s[:, n_loc:]], axis=1)
                m = jnp.max(s, axis=-1, keepdims=True)
                e = jnp.exp2(s - m)
                denom = jnp.sum(e, axis=-1, keepdims=True)
                outs.append(_dot(e.astype(BF16), jnp.concatenate(vals, axis=0)) / denom)
            o_ref[p, qrows, :] = jnp.concatenate(outs, axis=1).astype(o_ref.dtype)


def _neighbourhood_attention(q, kt, v, ktx, vx, bias, masks, layer, batch, ctx_len):
    assert ctx_len == ATT_TQ
    n = q.shape[1]
    nblk = n // batch // ATT_TQ
    nstep = nblk // ATT_SUB
    wide = lambda b, i: b * nstep + i
    prev = lambda b, i: b * nblk + jnp.maximum(ATT_SUB * i - 1, 0)
    nxt = lambda b, i: b * nblk + jnp.minimum(ATT_SUB * (i + 1), nblk - 1)
    rows = lambda f, w: pl.BlockSpec((HEAD_PAIRS, w * ATT_TQ, V7X_LANES), lambda b, i: (0, f(b, i), 0))
    cols = lambda f, w: pl.BlockSpec((NA_WIDTH, w * ATT_TQ), lambda b, i: (0, f(b, i)))
    comb = (masks.shape[0],) + bias.shape[1:]
    vmem = ((int(np.prod(comb)) + bias[0].size + masks.size) * 4 + 16 * (ATT_SUB + 3) * ATT_TQ * NA_WIDTH * 2
            + 48 * ATT_TQ * ATT_TQ * 4)
    return pl.pallas_call(
        _na_kernel,
        out_shape=jax.ShapeDtypeStruct((HEAD_PAIRS, n, V7X_LANES), BF16),
        grid=(batch, nstep),
        in_specs=[rows(wide, ATT_SUB), cols(prev, 1), cols(wide, ATT_SUB), cols(nxt, 1),
                  rows(prev, 1), rows(wide, ATT_SUB), rows(nxt, 1),
                  cols(lambda b, i: b, 1), rows(lambda b, i: b, 1),
                  _resident(bias.shape[1:], (layer,)), _resident(masks.shape)],
        out_specs=rows(wide, ATT_SUB),
        scratch_shapes=[pltpu.VMEM(comb, F32)],
        compiler_params=pltpu.CompilerParams(
            dimension_semantics=("arbitrary", "arbitrary"), vmem_limit_bytes=_vmem_limit(vmem)),
        name="neighbourhood_attention",
    )(q, kt, kt, kt, v, v, v, ktx, vx, bias, masks)


def _ctx_attn_kernel(q_ref, kt_ref, v_ref, o_ref):
    for p in range(HEAD_PAIRS):
        outs = []
        for hh in range(2):
            lanes = slice(hh * HEAD_DIM, (hh + 1) * HEAD_DIM)
            s = _dot(q_ref[p][:, lanes], kt_ref[pl.ds(p * V7X_LANES + hh * HEAD_DIM, HEAD_DIM), :])
            outs.append(_softmax_pv([s], [v_ref[p][:, lanes]]))
        o_ref[p] = jnp.concatenate(outs, axis=1).astype(o_ref.dtype)


def _context_attention(q, kt, v, batch):
    n = q.shape[1]
    spec = pl.BlockSpec((HEAD_PAIRS, n // batch, V7X_LANES), lambda b: (0, b, 0))
    return pl.pallas_call(
        _ctx_attn_kernel,
        out_shape=jax.ShapeDtypeStruct((HEAD_PAIRS, n, V7X_LANES), BF16),
        grid=(batch,),
        in_specs=[spec, pl.BlockSpec((NA_WIDTH, n // batch), lambda b: (0, b)), spec],
        out_specs=spec,
        compiler_params=pltpu.CompilerParams(dimension_semantics=("arbitrary",)),
        name="context_attention",
    )(q, kt, v)


def _rope_tables(seq):
    t = np.arange(seq)
    inv = ROPE_THETA ** (-np.arange(ROPE_PAIRS, dtype=np.float64) / ROPE_PAIRS)
    ang_r = (t // GRID_W)[:, None] * inv
    ang_c = (t % GRID_W)[:, None] * inv
    ang = np.concatenate([ang_r, ang_r, ang_c, ang_c], axis=1)
    first = np.tile(np.arange(HEAD_DIM) % (2 * ROPE_PAIRS) < ROPE_PAIRS, 2)
    cos = np.tile(np.cos(ang), (1, 2))
    sin = np.tile(np.sin(ang), (1, 2))
    sa = np.where(first, -sin, 0.0)
    sb = np.where(first, 0.0, sin)
    return tuple(jnp.asarray(a, F32) for a in (cos, sa, sb))


def _identity_rope_tables(n):
    ones = jnp.ones((n, V7X_LANES), F32)
    zeros = jnp.zeros((n, V7X_LANES), F32)
    return ones, zeros, zeros


def _pool_tables(tm, tb, length):
    assert tb % length == 0 and tm % tb == 0
    t = np.arange(tm)
    pos = t % length
    base = t - pos
    bands, icnts = [], []
    for w in POOL_WINDOWS:
        lo = np.clip(pos - w // 2, 0, length)
        hi = np.clip(pos - w // 2 + w, 0, length)
        s = t[None, :tb]
        bands.append((s >= (base + lo)[:tb, None]) & (s < (base + hi)[:tb, None]))
        icnts.append(np.repeat((1.0 / (hi - lo))[:, None], POOL_CH, axis=1))
    band = jnp.asarray(np.stack(bands), BF16)
    icnt = jnp.asarray(np.concatenate(icnts, axis=1), F32)
    return band, icnt


def _attention_bias(rpb, rows):
    kh = min(NA_KH, rows)
    assert kh == NA_KH and ATT_ROWS == NA_KH // 2 and rows >= 3 * ATT_ROWS
    rpb = rpb * LOG2_E
    by_row = jnp.stack([rpb[:, :, ATT_ROWS - 1 - i:ATT_ROWS - 1 - i + 3 * ATT_ROWS] for i in range(ATT_ROWS)], axis=2)
    pad = GRID_W - NA_KW
    padded = jnp.pad(by_row, ((0, 0),) * 4 + ((pad, pad),))
    toep = jnp.stack([padded[..., GRID_W - 1 - j:2 * GRID_W - 1 - j] for j in range(GRID_W)], axis=4)
    base = jnp.transpose(toep, (0, 1, 2, 4, 3, 5)).reshape(rpb.shape[:2] + (ATT_TQ, 3 * ATT_TQ))

    i = np.arange(ATT_TQ) // GRID_W
    j = np.arange(ATT_TQ) % GRID_W
    m = np.arange(3 * ATT_TQ) // GRID_W
    c = np.arange(3 * ATT_TQ) % GRID_W
    col_start = np.clip(j - NA_KW // 2, 0, GRID_W - NA_KW)
    col_ok = (c[None, :] >= col_start[:, None]) & (c[None, :] < col_start[:, None] + NA_KW)
    masks = []
    for r0 in (0, ATT_ROWS, rows - ATT_ROWS):
        r = r0 + i
        kr = r0 - ATT_ROWS + m
        row_start = np.clip(r - kh // 2, 0, rows - kh)
        row_ok = (kr[None, :] >= row_start[:, None]) & (kr[None, :] < row_start[:, None] + kh)
        masks.append(np.where(row_ok & col_ok, 0.0, NEG_INF))
    return base.astype(F32), jnp.asarray(np.stack(masks), F32)


def kernel(x, c, ctx, c_ctx, w_mod, b_mod, norm_g, w_ffn_gate_up, w_ffn_down, w_in, w_out, na_rpb, w_pool, pool_scale):
    batch, seq, d = x.shape
    ctx_len = ctx.shape[1]
    depth = w_mod.shape[0]
    rows = seq // GRID_W
    tm_x, tm_c = 512, ctx_len
    tm_ffn = 1024

    cv = jnp.concatenate([c, c_ctx[None], jnp.zeros((V7X_SUBLANES - batch - 1, d), F32)])
    mods = _mod_vectors(cv, w_mod, b_mod).reshape(depth, V7X_SUBLANES, N_MOD, 1, d)

    wgu, wdn = w_ffn_gate_up.astype(BF16), w_ffn_down.astype(BF16)
    w_in_b, w_out_b, w_pool_b = w_in.astype(BF16), w_out.astype(BF16), w_pool.astype(BF16)
    norm = norm_g.reshape(depth, norm_g.shape[1], 1, d)
    ps = pool_scale.reshape(depth, 1, POOL_WIDTH)
    bias, masks = _attention_bias(na_rpb, rows)

    rope_x = _rope_tables(seq)
    rope_c = _identity_rope_tables(tm_c)
    band_x, icnt_x = _pool_tables(tm_x, 2 * GRID_W, GRID_W)
    band_c, icnt_c = _pool_tables(tm_c, ctx_len, ctx_len)

    xt = x.reshape(batch * seq, d)
    ct = ctx.reshape(batch * ctx_len, d)
    for l in range(depth):
        last = l == depth - 1
        cx = _Cond(mods, norm, l, 0, batch)
        cc = _Cond(mods, norm, l, batch, 1)
        xt = _ffn(xt, cx, wgu, wdn, 0, tm_ffn)
        ct = _ffn(ct, cc, wgu, wdn, 0, tm_c)
        qx, ktx, vx, px = _inproj(xt, cx, w_in_b, rope_x, band_x, icnt_x, w_pool_b, ps, tm_x)
        qc, ktc, vc, pc = _inproj(ct, cc, w_in_b, rope_c, band_c, icnt_c, w_pool_b, ps, tm_c)
        ax = _neighbourhood_attention(qx, ktx, vx, ktc, vc, bias, masks, l, batch, ctx_len)
        xt = _ffn(xt, cx, wgu, wdn, 1, tm_ffn, mix=(ax, px, w_out_b))
        if not last:
            ac = _context_attention(qc, ktc, vc, batch)
            ct = _ffn(ct, cc, wgu, wdn, 1, tm_c, mix=(ac, pc, w_out_b))
    return xt.reshape(batch, seq, d)
```

```python
import functools

import numpy as np
import jax
import jax.numpy as jnp
from jax import lax
from jax.experimental import pallas as pl
from jax.experimental.pallas import tpu as pltpu

GRID_W = 64
N_MOD = 9
NA_HEADS = 8
HEAD_DIM = 64
NA_WIDTH = NA_HEADS * HEAD_DIM
NA_KH = 8
NA_KW = 16
POOL_GROUPS = 4
POOL_CH = 128
POOL_WIDTH = POOL_GROUPS * POOL_CH
POOL_WINDOWS = (2, 4, 8, 16)
ROPE_THETA = 10000.0
ROPE_PAIRS = HEAD_DIM // 4
RMS_EPS = 1e-6
NEG_INF = -1e30
LOG2_E = 1.4426950408889634

V7X_LANES = 128
V7X_SUBLANES = 8
V7X_VMEM_BYTES = 64 * 1024 * 1024

FF_CHUNK = 256
ATT_ROWS = 4
ATT_TQ = ATT_ROWS * GRID_W

F32 = jnp.float32
BF16 = jnp.bfloat16


def _vmem_limit(nbytes):
    return int(min(nbytes + (8 << 20), V7X_VMEM_BYTES - (6 << 20)))


def _resident(block_shape, lead=()):
    index = tuple(lead) + (0,) * len(block_shape)
    return pl.BlockSpec((None,) * len(lead) + tuple(block_shape), lambda *_: index, pipeline_mode=pl.Buffered(1))


def _rms(x, g):
    return x * lax.rsqrt(jnp.mean(x * x, axis=-1, keepdims=True) + RMS_EPS) * g


def _dot(a, b):
    return jnp.dot(a, b, preferred_element_type=F32)


def _split_bf16(x):
    hi = x.astype(BF16)
    lo = (x - hi.astype(F32)).astype(BF16)
    return hi, lo


def _mod_kernel(cv_ref, w_ref, b_ref, o_ref):
    s = cv_ref[...]
    s = s * jax.nn.sigmoid(s)
    s_hi, s_lo = _split_bf16(s)
    w_hi, w_lo = _split_bf16(w_ref[...])
    n = s.shape[0]
    r = _dot(jnp.concatenate([s_hi, s_lo], axis=0), w_hi)
    o_ref[...] = r[:n] + r[n:] + _dot(s_hi, w_lo) + b_ref[...]


def _mod_vectors(cv, w_mod, b_mod):
    depth, d, n = w_mod.shape
    tn = 1536
    return pl.pallas_call(
        _mod_kernel,
        out_shape=jax.ShapeDtypeStruct((depth, cv.shape[0], n), F32),
        grid=(depth, n // tn),
        in_specs=[
            pl.BlockSpec(cv.shape, lambda l, j: (0, 0)),
            pl.BlockSpec((None, d, tn), lambda l, j: (l, 0, j)),
            pl.BlockSpec((None, 1, tn), lambda l, j: (l, 0, j)),
        ],
        out_specs=pl.BlockSpec((None, cv.shape[0], tn), lambda l, j: (l, 0, j)),
        compiler_params=pltpu.CompilerParams(
            dimension_semantics=("arbitrary", "arbitrary"),
            vmem_limit_bytes=_vmem_limit(5 * d * tn * 4)),
        name="mod_vectors",
    )(cv, w_mod, b_mod.reshape(depth, 1, n))


class _Cond:
    def __init__(self, mods, norm, layer, row0, n_rows):
        self.mods, self.norm, self.layer, self.row0, self.n_rows = mods, norm, layer, row0, n_rows

    def specs(self, n_tiles):
        per_row = n_tiles // self.n_rows
        layer, row0 = self.layer, self.row0
        return [pl.BlockSpec((None, None) + self.mods.shape[2:], lambda i: (layer, row0 + i // per_row, 0, 0, 0)),
                pl.BlockSpec((None,) + self.norm.shape[1:], lambda i: (layer, 0, 0, 0))]


def _ffn_kernel(*refs, mixed, m0, g0):
    if mixed:
        x_ref, a_ref, p_ref, wo_ref, mod_ref, g_ref, wgu_ref, wd_ref, o_ref, h_scr, acc_scr = refs
        att = jnp.concatenate([a_ref[c] for c in range(a_ref.shape[0])], axis=1)
        y = _dot(att, wo_ref[pl.ds(0, NA_WIDTH), :]) + _dot(p_ref[...], wo_ref[pl.ds(NA_WIDTH, POOL_WIDTH), :])
        o_ref[...] = x_ref[...] + _rms(y, mod_ref[m0] * g_ref[g0])
        x_ref, m0, g0 = o_ref, m0 + 1, g0 + 1
    else:
        x_ref, mod_ref, g_ref, wgu_ref, wd_ref, o_ref, h_scr, acc_scr = refs
    mod_ref, g_ref = mod_ref.at[pl.ds(m0, 3)], g_ref.at[pl.ds(g0, 2)]
    h = _rms(x_ref[...], g_ref[0] * (1.0 + mod_ref[1])) + mod_ref[0]
    h_scr[...] = h.astype(BF16)
    d_ff = wd_ref.shape[0]
    for j in range(d_ff // FF_CHUNK):
        hb = h_scr[...]
        gate = _dot(hb, wgu_ref[:, pl.ds(j * FF_CHUNK, FF_CHUNK)])
        up = _dot(hb, wgu_ref[:, pl.ds(d_ff + j * FF_CHUNK, FF_CHUNK)])
        act = (gate * jax.nn.sigmoid(gate) * up).astype(BF16)
        part = _dot(act, wd_ref[pl.ds(j * FF_CHUNK, FF_CHUNK), :])
        if j == 0:
            acc_scr[...] = part
        else:
            acc_scr[...] += part
    o_ref[...] = x_ref[...] + _rms(acc_scr[...], 0.5 * mod_ref[2] * g_ref[1])


def _ffn(xt, cond, wgu, wd, half, tm, mix=None):
    n, d = xt.shape
    nb = n // tm
    which = (cond.layer, half)
    d_ff = wd.shape[2]
    assert d_ff % FF_CHUNK == 0 and wgu.shape[2:] == (d, 2 * d_ff)
    tok = pl.BlockSpec((tm, d), lambda i: (i, 0))
    vmem = 4 * tm * d * 4 + tm * d * 2 + tm * d * 4 + 3 * d * d_ff * 2 + 4 * tm * FF_CHUNK * 4
    operands, specs = [xt], [tok]
    if mix is not None:
        att, pool, w_out = mix
        operands += [att, pool, w_out]
        specs += [pl.BlockSpec((att.shape[0], tm, att.shape[2]), lambda i: (0, i, 0)),
                  pl.BlockSpec((tm, pool.shape[1]), lambda i: (i, 0)), _resident(w_out.shape[1:], (cond.layer,))]
        vmem += 4 * tm * pool.shape[1] * 2 + 2 * d * d * 2 + 2 * tm * d * 4
    operands += [cond.mods, cond.norm, wgu, wd]
    specs += cond.specs(nb) + [_resident(wgu.shape[2:], which), _resident(wd.shape[2:], which)]
    m0, g0 = (5, 3) if mix is not None else (3 * 2 * half, 4 * half)
    return pl.pallas_call(
        functools.partial(_ffn_kernel, mixed=mix is not None, m0=m0, g0=g0),
        out_shape=jax.ShapeDtypeStruct((n, d), F32),
        grid=(nb,),
        in_specs=specs,
        out_specs=tok,
        scratch_shapes=[pltpu.VMEM((tm, d), BF16), pltpu.VMEM((tm, d), F32)],
        compiler_params=pltpu.CompilerParams(
            dimension_semantics=("arbitrary",), vmem_limit_bytes=_vmem_limit(vmem)),
        name="mix_ffn" if mix is not None else "sandwich_ffn",
    )(*operands)


def _inproj_kernel(x_ref, mod_ref, g_ref, w_ref, cos_ref, sa_ref, sb_ref, band_ref, icnt_ref,
                   wp_ref, ps_ref, q_ref, kt_ref, v_ref, p_ref):
    h = (_rms(x_ref[...], g_ref[2] * (1.0 + mod_ref[4])) + mod_ref[3]).astype(BF16)
    cos, sa, sb = cos_ref[...], sa_ref[...], sb_ref[...]

    def rope(t):
        up = pltpu.roll(t, V7X_LANES - ROPE_PAIRS, 1)
        down = pltpu.roll(t, ROPE_PAIRS, 1)
        return t * cos + up * sa + down * sb

    scale = HEAD_DIM ** -0.5 * LOG2_E
    qk =_dot(h, w_ref[:, pl.ds(0, 2 * NA_WIDTH)])
    for c in range(NA_WIDTH // V7X_LANES):
        lanes = pl.ds(c * V7X_LANES, V7X_LANES)
        q_ref[c] = (rope(qk[:, c * V7X_LANES:(c + 1) * V7X_LANES]) * scale).astype(BF16)
        k_rot = rope(qk[:, NA_WIDTH + c * V7X_LANES:NA_WIDTH + (c + 1) * V7X_LANES])
        kt_ref[lanes, :] = k_rot.T.astype(BF16)
    vu = _dot(h, w_ref[:, pl.ds(2 * NA_WIDTH, NA_WIDTH + POOL_WIDTH)])
    for c in range(NA_WIDTH // V7X_LANES):
        v_ref[c] = vu[:, c * V7X_LANES:(c + 1) * V7X_LANES].astype(BF16)
    tm, tb = x_ref.shape[0], band_ref.shape[1]
    for g in range(POOL_GROUPS):
        lanes = pl.ds(g * POOL_CH, POOL_CH)
        u = vu[:, NA_WIDTH + g * POOL_CH:NA_WIDTH + (g + 1) * POOL_CH]
        u2 = jnp.concatenate(_split_bf16(u), axis=1)
        band = band_ref[g]
        win2 = jnp.concatenate([_dot(band, u2[s * tb:(s + 1) * tb]) for s in range(tm // tb)], axis=0)
        win = win2[:, :POOL_CH] + win2[:, POOL_CH:]
        dlt = (win * icnt_ref[:, lanes] - u).astype(BF16)
        p_ref[:, lanes] = (_dot(dlt, wp_ref[g]) * ps_ref[:, lanes]).astype(BF16)


def _inproj(xt, cond, w_in, rope_tabs, band, icnt, w_pool, pool_scale, tm):
    n, d = xt.shape
    nb = n // tm
    layer = (cond.layer,)
    tab_blocks = rope_tabs[0].shape[0] // tm
    tab_spec = pl.BlockSpec((tm, V7X_LANES), lambda i: (i % tab_blocks, 0))
    out = jax.ShapeDtypeStruct((n, NA_WIDTH), BF16)
    out_spec = pl.BlockSpec((tm, NA_WIDTH), lambda i: (i, 0))
    vmem = (2 * tm * d * 4 + tm * d * 2 + w_in[0].size * 2 + 6 * tm * V7X_LANES * 4 + band.size * 2
            + icnt.size * 4 + 8 * tm * NA_WIDTH * 2 + 6 * tm * NA_WIDTH * 4)
    out_t = jax.ShapeDtypeStruct((NA_WIDTH, n), BF16)
    out_t_spec = pl.BlockSpec((NA_WIDTH, tm), lambda i: (0, i))
    pairs = NA_WIDTH // V7X_LANES
    out_p = jax.ShapeDtypeStruct((pairs, n, V7X_LANES), BF16)
    out_p_spec = pl.BlockSpec((pairs, tm, V7X_LANES), lambda i: (0, i, 0))
    return pl.pallas_call(
        _inproj_kernel,
        out_shape=(out_p, out_t, out_p, out),
        grid=(nb,),
        in_specs=[pl.BlockSpec((tm, d), lambda i: (i, 0))] + cond.specs(nb) + [
            _resident(w_in.shape[1:], layer),
            tab_spec, tab_spec, tab_spec,
            _resident(band.shape), _resident(icnt.shape), _resident(w_pool.shape[1:], layer),
            _resident(pool_scale.shape[1:], layer),
        ],
        out_specs=(out_p_spec, out_t_spec, out_p_spec, out_spec),
        compiler_params=pltpu.CompilerParams(
            dimension_semantics=("arbitrary",), vmem_limit_bytes=_vmem_limit(vmem)),
        name="in_projection",
    )(xt, cond.mods, cond.norm, w_in, *rope_tabs, band, icnt, w_pool, pool_scale)


def _softmax_pv(scores, values):
    assert len({s.shape for s in scores}) == 1
    m = jnp.max(functools.reduce(jnp.maximum, scores), axis=-1, keepdims=True)
    probs = [jnp.exp2(s - m) for s in scores]
    denom = jnp.sum(functools.reduce(lambda a, b: a + b, probs), axis=-1, keepdims=True)
    o = functools.reduce(lambda a, b: a + b, [_dot(p.astype(BF16), v) for p, v in zip(probs, values)])
    return o / denom


HEAD_PAIRS = NA_WIDTH // V7X_LANES


ATT_SUB = 2


def _na_kernel(q_ref, kp_ref, km_ref, kn_ref, vp_ref, vm_ref, vn_ref, kx_ref, vx_ref, bias_ref, mask_ref, o_ref,
               comb_scr, s_scr):
    i, last = pl.program_id(1), pl.num_programs(1) - 1

    @pl.when((pl.program_id(0) == 0) & (i == 0))
    def _():
        for var in range(comb_scr.shape[0]):
            for h in range(NA_HEADS):
                comb_scr[var, h] = bias_ref[h] + mask_ref[var]

    n_loc = 3 * ATT_TQ
    for sub in range(ATT_SUB):
        variant = 1
        if sub == 0:
            variant = jnp.where(i == 0, 0, variant)
        if sub == ATT_SUB - 1:
            variant = jnp.where(i == last, 2, variant)
        qrows = slice(sub * ATT_TQ, (sub + 1) * ATT_TQ)
        mid = slice(max(sub - 1, 0) * ATT_TQ, min(sub + 2, ATT_SUB) * ATT_TQ)
        for p in range(HEAD_PAIRS):
            for hh in range(2):
                h = 2 * p + hh
                lanes = slice(hh * HEAD_DIM, (hh + 1) * HEAD_DIM)
                feat = pl.ds(h * HEAD_DIM, HEAD_DIM)
                kts = ([kp_ref[feat, :]] if sub == 0 else []) + [km_ref[feat, mid]]
                kts += ([kn_ref[feat, :]] if sub == ATT_SUB - 1 else []) + [kx_ref[feat, :]]
                s = _dot(q_ref[p][qrows, lanes], jnp.concatenate(kts, axis=1))
                s_scr[h] = jnp.concatenate([s[:, :n_loc] + comb_scr[variant, h], s[:, n_loc:]], axis=1)
        for p in range(HEAD_PAIRS):
            outs = []
            for hh in range(2):
                h = 2 * p + hh
                lanes = slice(hh * HEAD_DIM, (hh + 1) * HEAD_DIM)
                vals = ([vp_ref[p][:, lanes]] if sub == 0 else []) + [vm_ref[p][mid, lanes]]
                vals += ([vn_ref[p][:, lanes]] if sub == ATT_SUB - 1 else []) + [vx_ref[p][:, lanes]]
                s = s_scr[h]
                m = jnp.max(s, axis=-1, keepdims=True)
                e = jnp.exp2(s - m)
                denom = jnp.sum(e, axis=-1, keepdims=True)
                outs.append(_dot(e.astype(BF16), jnp.concatenate(vals, axis=0)) / denom)
            o_ref[p, qrows, :] = jnp.concatenate(outs, axis=1).astype(o_ref.dtype)


def _neighbourhood_attention(q, kt, v, ktx, vx, bias, masks, layer, batch, ctx_len):
    assert ctx_len == ATT_TQ
    n = q.shape[1]
    nblk = n // batch // ATT_TQ
    nstep = nblk // ATT_SUB
    wide = lambda b, i: b * nstep + i
    prev = lambda b, i: b * nblk + jnp.maximum(ATT_SUB * i - 1, 0)
    nxt = lambda b, i: b * nblk + jnp.minimum(ATT_SUB * (i + 1), nblk - 1)
    rows = lambda f, w: pl.BlockSpec((HEAD_PAIRS, w * ATT_TQ, V7X_LANES), lambda b, i: (0, f(b, i), 0))
    cols = lambda f, w: pl.BlockSpec((NA_WIDTH, w * ATT_TQ), lambda b, i: (0, f(b, i)))
    comb = (masks.shape[0],) + bias.shape[1:]
    vmem = ((int(np.prod(comb)) + bias[0].size + masks.size) * 4 + 16 * (ATT_SUB + 3) * ATT_TQ * NA_WIDTH * 2
            + 48 * ATT_TQ * ATT_TQ * 4)
    return pl.pallas_call(
        _na_kernel,
        out_shape=jax.ShapeDtypeStruct((HEAD_PAIRS, n, V7X_LANES), BF16),
        grid=(batch, nstep),
        in_specs=[rows(wide, ATT_SUB), cols(prev, 1), cols(wide, ATT_SUB), cols(nxt, 1),
                  rows(prev, 1), rows(wide, ATT_SUB), rows(nxt, 1),
                  cols(lambda b, i: b, 1), rows(lambda b, i: b, 1),
                  _resident(bias.shape[1:], (layer,)), _resident(masks.shape)],
        out_specs=rows(wide, ATT_SUB),
        scratch_shapes=[pltpu.VMEM(comb, F32), pltpu.VMEM((NA_HEADS, ATT_TQ, 4 * ATT_TQ), F32)],
        compiler_params=pltpu.CompilerParams(
            dimension_semantics=("arbitrary", "arbitrary"), vmem_limit_bytes=_vmem_limit(vmem)),
        name="neighbourhood_attention",
    )(q, kt, kt, kt, v, v, v, ktx, vx, bias, masks)


def _ctx_attn_kernel(q_ref, kt_ref, v_ref, o_ref):
    for p in range(HEAD_PAIRS):
        outs = []
        for hh in range(2):
            lanes = slice(hh * HEAD_DIM, (hh + 1) * HEAD_DIM)
            s = _dot(q_ref[p][:, lanes], kt_ref[pl.ds(p * V7X_LANES + hh * HEAD_DIM, HEAD_DIM), :])
            outs.append(_softmax_pv([s], [v_ref[p][:, lanes]]))
        o_ref[p] = jnp.concatenate(outs, axis=1).astype(o_ref.dtype)


def _context_attention(q, kt, v, batch):
    n = q.shape[1]
    spec = pl.BlockSpec((HEAD_PAIRS, n // batch, V7X_LANES), lambda b: (0, b, 0))
    return pl.pallas_call(
        _ctx_attn_kernel,
        out_shape=jax.ShapeDtypeStruct((HEAD_PAIRS, n, V7X_LANES), BF16),
        grid=(batch,),
        in_specs=[spec, pl.BlockSpec((NA_WIDTH, n // batch), lambda b: (0, b)), spec],
        out_specs=spec,
        compiler_params=pltpu.CompilerParams(dimension_semantics=("arbitrary",)),
        name="context_attention",
    )(q, kt, v)


def _rope_tables(seq):
    t = np.arange(seq)
    inv = ROPE_THETA ** (-np.arange(ROPE_PAIRS, dtype=np.float64) / ROPE_PAIRS)
    ang_r = (t // GRID_W)[:, None] * inv
    ang_c = (t % GRID_W)[:, None] * inv
    ang = np.concatenate([ang_r, ang_r, ang_c, ang_c], axis=1)
    first = np.tile(np.arange(HEAD_DIM) % (2 * ROPE_PAIRS) < ROPE_PAIRS, 2)
    cos = np.tile(np.cos(ang), (1, 2))
    sin = np.tile(np.sin(ang), (1, 2))
    sa = np.where(first, -sin, 0.0)
    sb = np.where(first, 0.0, sin)
    return tuple(jnp.asarray(a, F32) for a in (cos, sa, sb))


def _identity_rope_tables(n):
    ones = jnp.ones((n, V7X_LANES), F32)
    zeros = jnp.zeros((n, V7X_LANES), F32)
    return ones, zeros, zeros


def _pool_tables(tm, tb, length):
    assert tb % length == 0 and tm % tb == 0
    t = np.arange(tm)
    pos = t % length
    base = t - pos
    bands, icnts = [], []
    for w in POOL_WINDOWS:
        lo = np.clip(pos - w // 2, 0, length)
        hi = np.clip(pos - w // 2 + w, 0, length)
        s = t[None, :tb]
        bands.append((s >= (base + lo)[:tb, None]) & (s < (base + hi)[:tb, None]))
        icnts.append(np.repeat((1.0 / (hi - lo))[:, None], POOL_CH, axis=1))
    band = jnp.asarray(np.stack(bands), BF16)
    icnt = jnp.asarray(np.concatenate(icnts, axis=1), F32)
    return band, icnt


def _attention_bias(rpb, rows):
    kh = min(NA_KH, rows)
    assert kh == NA_KH and ATT_ROWS == NA_KH // 2 and rows >= 3 * ATT_ROWS
    rpb = rpb * LOG2_E
    by_row = jnp.stack([rpb[:, :, ATT_ROWS - 1 - i:ATT_ROWS - 1 - i + 3 * ATT_ROWS] for i in range(ATT_ROWS)], axis=2)
    pad = GRID_W - NA_KW
    padded = jnp.pad(by_row, ((0, 0),) * 4 + ((pad, pad),))
    toep = jnp.stack([padded[..., GRID_W - 1 - j:2 * GRID_W - 1 - j] for j in range(GRID_W)], axis=4)
    base = jnp.transpose(toep, (0, 1, 2, 4, 3, 5)).reshape(rpb.shape[:2] + (ATT_TQ, 3 * ATT_TQ))

    i = np.arange(ATT_TQ) // GRID_W
    j = np.arange(ATT_TQ) % GRID_W
    m = np.arange(3 * ATT_TQ) // GRID_W
    c = np.arange(3 * ATT_TQ) % GRID_W
    col_start = np.clip(j - NA_KW // 2, 0, GRID_W - NA_KW)
    col_ok = (c[None, :] >= col_start[:, None]) & (c[None, :] < col_start[:, None] + NA_KW)
    masks = []
    for r0 in (0, ATT_ROWS, rows - ATT_ROWS):
        r = r0 + i
        kr = r0 - ATT_ROWS + m
        row_start = np.clip(r - kh // 2, 0, rows - kh)
        row_ok = (kr[None, :] >= row_start[:, None]) & (kr[None, :] < row_start[:, None] + kh)
        masks.append(np.where(row_ok & col_ok, 0.0, NEG_INF))
    return base.astype(F32), jnp.asarray(np.stack(masks), F32)


def kernel(x, c, ctx, c_ctx, w_mod, b_mod, norm_g, w_ffn_gate_up, w_ffn_down, w_in, w_out, na_rpb, w_pool, pool_scale):
    batch, seq, d = x.shape
    ctx_len = ctx.shape[1]
    depth = w_mod.shape[0]
    rows = seq // GRID_W
    tm_x, tm_c = 512, ctx_len
    tm_ffn = 1024

    cv = jnp.concatenate([c, c_ctx[None], jnp.zeros((V7X_SUBLANES - batch - 1, d), F32)])
    mods = _mod_vectors(cv, w_mod, b_mod).reshape(depth, V7X_SUBLANES, N_MOD, 1, d)

    wgu, wdn = w_ffn_gate_up.astype(BF16), w_ffn_down.astype(BF16)
    w_in_b, w_out_b, w_pool_b = w_in.astype(BF16), w_out.astype(BF16), w_pool.astype(BF16)
    norm = norm_g.reshape(depth, norm_g.shape[1], 1, d)
    ps = pool_scale.reshape(depth, 1, POOL_WIDTH)
    bias, masks = _attention_bias(na_rpb, rows)

    rope_x = _rope_tables(seq)
    rope_c = _identity_rope_tables(tm_c)
    band_x, icnt_x = _pool_tables(tm_x, 2 * GRID_W, GRID_W)
    band_c, icnt_c = _pool_tables(tm_c, ctx_len, ctx_len)

    xt = x.reshape(batch * seq, d)
    ct = ctx.reshape(batch * ctx_len, d)
    for l in range(depth):
        last = l == depth - 1
        cx = _Cond(mods, norm, l, 0, batch)
        cc = _Cond(mods, norm, l, batch, 1)
        xt = _ffn(xt, cx, wgu, wdn, 0, tm_ffn)
        ct = _ffn(ct, cc, wgu, wdn, 0, tm_c)
        qx, ktx, vx, px = _inproj(xt, cx, w_in_b, rope_x, band_x, icnt_x, w_pool_b, ps, tm_x)
        qc, ktc, vc, pc = _inproj(ct, cc, w_in_b, rope_c, band_c, icnt_c, w_pool_b, ps, tm_c)
        ax = _neighbourhood_attention(qx, ktx, vx, ktc, vc, bias, masks, l, batch, ctx_len)
        xt = _ffn(xt, cx, wgu, wdn, 1, tm_ffn, mix=(ax, px, w_out_b))
        if not last:
            ac = _context_attention(qc, ktc, vc, batch)
            ct = _ffn(ct, cc, wgu, wdn, 1, tm_c, mix=(ac, pc, w_out_b))
    return xt.reshape(batch, seq, d)
```

```python
import functools

import numpy as np
import jax
import jax.numpy as jnp
from jax import lax
from jax.experimental import pallas as pl
from jax.experimental.pallas import tpu as pltpu

GRID_W = 64
N_MOD = 9
NA_HEADS = 8
HEAD_DIM = 64
NA_WIDTH = NA_HEADS * HEAD_DIM
NA_KH = 8
NA_KW = 16
POOL_GROUPS = 4
POOL_CH = 128
POOL_WIDTH = POOL_GROUPS * POOL_CH
POOL_WINDOWS = (2, 4, 8, 16)
ROPE_THETA = 10000.0
ROPE_PAIRS = HEAD_DIM // 4
RMS_EPS = 1e-6
NEG_INF = -1e30
LOG2_E = 1.4426950408889634

V7X_LANES = 128
V7X_SUBLANES = 8
V7X_VMEM_BYTES = 64 * 1024 * 1024

HEAD_PAIRS = NA_WIDTH // V7X_LANES
FF_CHUNK = 256
ATT_ROWS = 4
ATT_TQ = ATT_ROWS * GRID_W

F32 = jnp.float32
BF16 = jnp.bfloat16


def _vmem_limit(nbytes):
    return int(min(nbytes + (8 << 20), V7X_VMEM_BYTES - (6 << 20)))


def _resident(block_shape, lead=()):
    index = tuple(lead) + (0,) * len(block_shape)
    return pl.BlockSpec((None,) * len(lead) + tuple(block_shape), lambda *_: index, pipeline_mode=pl.Buffered(1))


def _rms(x, g):
    return x * lax.rsqrt(jnp.mean(x * x, axis=-1, keepdims=True) + RMS_EPS) * g


def _dot(a, b):
    return jnp.dot(a, b, preferred_element_type=F32)


def _split_bf16(x):
    hi = x.astype(BF16)
    lo = (x - hi.astype(F32)).astype(BF16)
    return hi, lo


def _mod_kernel(cv_ref, w_ref, b_ref, o_ref):
    s = cv_ref[...]
    s = s * jax.nn.sigmoid(s)
    s_hi, s_lo = _split_bf16(s)
    w_hi, w_lo = _split_bf16(w_ref[...])
    n = s.shape[0]
    r = _dot(jnp.concatenate([s_hi, s_lo], axis=0), w_hi)
    o_ref[...] = r[:n] + r[n:] + _dot(s_hi, w_lo) + b_ref[...]


def _mod_vectors(cv, w_mod, b_mod):
    depth, d, n = w_mod.shape
    tn = 1536
    return pl.pallas_call(
        _mod_kernel,
        out_shape=jax.ShapeDtypeStruct((depth, cv.shape[0], n), F32),
        grid=(depth, n // tn),
        in_specs=[
            pl.BlockSpec(cv.shape, lambda l, j: (0, 0)),
            pl.BlockSpec((None, d, tn), lambda l, j: (l, 0, j)),
            pl.BlockSpec((None, 1, tn), lambda l, j: (l, 0, j)),
        ],
        out_specs=pl.BlockSpec((None, cv.shape[0], tn), lambda l, j: (l, 0, j)),
        compiler_params=pltpu.CompilerParams(
            dimension_semantics=("arbitrary", "arbitrary"),
            vmem_limit_bytes=_vmem_limit(5 * d * tn * 4)),
        name="mod_vectors",
    )(cv, w_mod, b_mod.reshape(depth, 1, n))


class _Cond:
    def __init__(self, mods, norm, layer, row0, n_rows):
        self.mods, self.norm, self.layer, self.row0, self.n_rows = mods, norm, layer, row0, n_rows

    def specs(self, n_tiles):
        per_row = n_tiles // self.n_rows
        layer, row0 = self.layer, self.row0
        return [pl.BlockSpec((None, None) + self.mods.shape[2:], lambda i: (layer, row0 + i // per_row, 0, 0, 0)),
                pl.BlockSpec((None,) + self.norm.shape[1:], lambda i: (layer, 0, 0, 0))]


def _ffn_kernel(*refs, mixed, projected, m0, g0):
    refs = list(refs)
    x_ref = refs.pop(0)
    mix_refs = [refs.pop(0) for _ in range(3)] if mixed else None
    mod_all, g_all, wgu_ref, wd_ref = (refs.pop(0) for _ in range(4))
    proj_in = [refs.pop(0) for _ in range(8)] if projected else None
    o_ref = refs.pop(0)
    proj_out = [refs.pop(0) for _ in range(4)] if projected else None
    h_scr, acc_scr = refs
    if mixed:
        a_ref, p_ref, wo_ref = mix_refs
        att = jnp.concatenate([a_ref[c] for c in range(a_ref.shape[0])], axis=1)
        y = _dot(att, wo_ref[pl.ds(0, NA_WIDTH), :]) + _dot(p_ref[...], wo_ref[pl.ds(NA_WIDTH, POOL_WIDTH), :])
        o_ref[...] = x_ref[...] + _rms(y, mod_all[m0] * g_all[g0])
        x_ref, m0, g0 = o_ref, m0 + 1, g0 + 1
    mod_ref, g_ref = mod_all.at[pl.ds(m0, 3)], g_all.at[pl.ds(g0, 2)]
    h = _rms(x_ref[...], g_ref[0] * (1.0 + mod_ref[1])) + mod_ref[0]
    h_scr[...] = h.astype(BF16)
    d_ff = wd_ref.shape[0]
    for j in range(d_ff // FF_CHUNK):
        hb = h_scr[...]
        gate = _dot(hb, wgu_ref[:, pl.ds(j * FF_CHUNK, FF_CHUNK)])
        up = _dot(hb, wgu_ref[:, pl.ds(d_ff + j * FF_CHUNK, FF_CHUNK)])
        act = (gate * jax.nn.sigmoid(gate) * up).astype(BF16)
        part = _dot(act, wd_ref[pl.ds(j * FF_CHUNK, FF_CHUNK), :])
        if j == 0:
            acc_scr[...] = part
        else:
            acc_scr[...] += part
    o_ref[...] = x_ref[...] + _rms(acc_scr[...], 0.5 * mod_ref[2] * g_ref[1])
    if projected:
        _token_mix_inputs(o_ref, mod_all, g_all, *proj_in, *proj_out)


def _ffn(xt, cond, wgu, wd, half, tm, mix=None, proj=None):
    n, d = xt.shape
    nb = n // tm
    layer = (cond.layer,)
    which = (cond.layer, half)
    d_ff = wd.shape[2]
    assert d_ff % FF_CHUNK == 0 and wgu.shape[2:] == (d, 2 * d_ff)
    tok = pl.BlockSpec((tm, d), lambda i: (i, 0))
    vmem = 4 * tm * d * 4 + tm * d * 2 + tm * d * 4 + 3 * d * d_ff * 2 + 4 * tm * FF_CHUNK * 4
    operands, specs = [xt], [tok]
    if mix is not None:
        att, pool, w_out = mix
        operands += [att, pool, w_out]
        specs += [pl.BlockSpec((att.shape[0], tm, att.shape[2]), lambda i: (0, i, 0)),
                  pl.BlockSpec((tm, pool.shape[1]), lambda i: (i, 0)), _resident(w_out.shape[1:], (cond.layer,))]
        vmem += 4 * tm * pool.shape[1] * 2 + 2 * d * d * 2 + 2 * tm * d * 4
    operands += [cond.mods, cond.norm, wgu, wd]
    specs += cond.specs(nb) + [_resident(wgu.shape[2:], which), _resident(wd.shape[2:], which)]
    out_shape, out_specs = [jax.ShapeDtypeStruct((n, d), F32)], [tok]
    if proj is not None:
        w_in, rope_tabs, band, icnt, w_pool, pool_scale = proj
        tab_blocks = rope_tabs[0].shape[0] // tm
        tab_spec = pl.BlockSpec((tm, V7X_LANES), lambda i: (i % tab_blocks, 0))
        operands += [w_in, *rope_tabs, band, icnt, w_pool, pool_scale]
        specs += [_resident(w_in.shape[1:], layer), tab_spec, tab_spec, tab_spec, _resident(band.shape),
                  _resident(icnt.shape), _resident(w_pool.shape[1:], layer), _resident(pool_scale.shape[1:], layer)]
        pair = jax.ShapeDtypeStruct((HEAD_PAIRS, n, V7X_LANES), BF16)
        pair_spec = pl.BlockSpec((HEAD_PAIRS, tm, V7X_LANES), lambda i: (0, i, 0))
        out_shape += [pair, jax.ShapeDtypeStruct((NA_WIDTH, n), BF16), pair, jax.ShapeDtypeStruct((n, POOL_WIDTH), BF16)]
        out_specs += [pair_spec, pl.BlockSpec((NA_WIDTH, tm), lambda i: (0, i)), pair_spec,
                      pl.BlockSpec((tm, POOL_WIDTH), lambda i: (i, 0))]
        vmem += (w_in[0].size * 2 + 6 * tm * V7X_LANES * 4 + band.size * 2 + icnt.size * 4
                 + 8 * tm * NA_WIDTH * 2 + 6 * tm * NA_WIDTH * 4)
    m0, g0 = (5, 3) if mix is not None else (3 * 2 * half, 4 * half)
    outs = pl.pallas_call(
        functools.partial(_ffn_kernel, mixed=mix is not None, projected=proj is not None, m0=m0, g0=g0),
        out_shape=out_shape,
        grid=(nb,),
        in_specs=specs,
        out_specs=out_specs,
        scratch_shapes=[pltpu.VMEM((tm, d), BF16), pltpu.VMEM((tm, d), F32)],
        compiler_params=pltpu.CompilerParams(
            dimension_semantics=("arbitrary",), vmem_limit_bytes=_vmem_limit(vmem)),
        name="mix_ffn" if mix is not None else ("ffn_proj" if proj is not None else "sandwich_ffn"),
    )(*operands)
    return outs if proj is not None else outs[0]


def _token_mix_inputs(x_ref, mod_ref, g_ref, w_ref, cos_ref, sa_ref, sb_ref, band_ref, icnt_ref, wp_ref, ps_ref,
                      q_ref, kt_ref, v_ref, p_ref):
    h = (_rms(x_ref[...], g_ref[2] * (1.0 + mod_ref[4])) + mod_ref[3]).astype(BF16)
    cos, sa, sb = cos_ref[...], sa_ref[...], sb_ref[...]

    def rope(t):
        up = pltpu.roll(t, V7X_LANES - ROPE_PAIRS, 1)
        down = pltpu.roll(t, ROPE_PAIRS, 1)
        return t * cos + up * sa + down * sb

    scale = HEAD_DIM ** -0.5 * LOG2_E
    qk =_dot(h, w_ref[:, pl.ds(0, 2 * NA_WIDTH)])
    for c in range(NA_WIDTH // V7X_LANES):
        lanes = pl.ds(c * V7X_LANES, V7X_LANES)
        q_ref[c] = (rope(qk[:, c * V7X_LANES:(c + 1) * V7X_LANES]) * scale).astype(BF16)
        k_rot = rope(qk[:, NA_WIDTH + c * V7X_LANES:NA_WIDTH + (c + 1) * V7X_LANES])
        kt_ref[lanes, :] = k_rot.T.astype(BF16)
    vu = _dot(h, w_ref[:, pl.ds(2 * NA_WIDTH, NA_WIDTH + POOL_WIDTH)])
    for c in range(NA_WIDTH // V7X_LANES):
        v_ref[c] = vu[:, c * V7X_LANES:(c + 1) * V7X_LANES].astype(BF16)
    tm, tb = x_ref.shape[0], band_ref.shape[1]
    for g in range(POOL_GROUPS):
        lanes = pl.ds(g * POOL_CH, POOL_CH)
        u = vu[:, NA_WIDTH + g * POOL_CH:NA_WIDTH + (g + 1) * POOL_CH]
        u2 = jnp.concatenate(_split_bf16(u), axis=1)
        band = band_ref[g]
        win2 = jnp.concatenate([_dot(band, u2[s * tb:(s + 1) * tb]) for s in range(tm // tb)], axis=0)
        win = win2[:, :POOL_CH] + win2[:, POOL_CH:]
        dlt = (win * icnt_ref[:, lanes] - u).astype(BF16)
        p_ref[:, lanes] = (_dot(dlt, wp_ref[g]) * ps_ref[:, lanes]).astype(BF16)


def _softmax_pv(scores, values):
    assert len({s.shape for s in scores}) == 1
    m = jnp.max(functools.reduce(jnp.maximum, scores), axis=-1, keepdims=True)
    probs = [jnp.exp2(s - m) for s in scores]
    denom = jnp.sum(functools.reduce(lambda a, b: a + b, probs), axis=-1, keepdims=True)
    o = functools.reduce(lambda a, b: a + b, [_dot(p.astype(BF16), v) for p, v in zip(probs, values)])
    return o / denom


ATT_SUB = 2


def _na_kernel(q_ref, kp_ref, km_ref, kn_ref, vp_ref, vm_ref, vn_ref, kx_ref, vx_ref, bias_ref, mask_ref, o_ref,
               comb_scr, s_scr):
    i, last = pl.program_id(1), pl.num_programs(1) - 1

    @pl.when((pl.program_id(0) == 0) & (i == 0))
    def _():
        for var in range(comb_scr.shape[0]):
            for h in range(NA_HEADS):
                comb_scr[var, h] = bias_ref[h] + mask_ref[var]

    n_loc = 3 * ATT_TQ
    mids = [slice(max(sub - 1, 0) * ATT_TQ, min(sub + 2, ATT_SUB) * ATT_TQ) for sub in range(ATT_SUB)]
    for sub in range(ATT_SUB):
        variant = 1
        if sub == 0:
            variant = jnp.where(i == 0, 0, variant)
        if sub == ATT_SUB - 1:
            variant = jnp.where(i == last, 2, variant)
        for h in range(NA_HEADS):
            p, lanes = h // 2, slice((h % 2) * HEAD_DIM, (h % 2 + 1) * HEAD_DIM)
            feat = pl.ds(h * HEAD_DIM, HEAD_DIM)
            kts = ([kp_ref[feat, :]] if sub == 0 else []) + [km_ref[feat, mids[sub]]]
            kts += ([kn_ref[feat, :]] if sub == ATT_SUB - 1 else []) + [kx_ref[feat, :]]
            s = _dot(q_ref[p][sub * ATT_TQ:(sub + 1) * ATT_TQ, lanes], jnp.concatenate(kts, axis=1))
            s_scr[h] = jnp.concatenate([s[:, :n_loc] + comb_scr[variant, h], s[:, n_loc:]], axis=1)
        for p in range(HEAD_PAIRS):
            outs = []
            for hh in range(2):
                lanes = slice(hh * HEAD_DIM, (hh + 1) * HEAD_DIM)
                vals = ([vp_ref[p][:, lanes]] if sub == 0 else []) + [vm_ref[p][mids[sub], lanes]]
                vals += ([vn_ref[p][:, lanes]] if sub == ATT_SUB - 1 else []) + [vx_ref[p][:, lanes]]
                s = s_scr[2 * p + hh]
                m = jnp.max(s, axis=-1, keepdims=True)
                e = jnp.exp2(s - m)
                denom = jnp.sum(e, axis=-1, keepdims=True)
                outs.append(_dot(e.astype(BF16), jnp.concatenate(vals, axis=0)) / denom)
            o_ref[p, sub * ATT_TQ:(sub + 1) * ATT_TQ, :] = jnp.concatenate(outs, axis=1).astype(o_ref.dtype)


def _neighbourhood_attention(q, kt, v, ktx, vx, bias, masks, layer, batch, ctx_len):
    assert ctx_len == ATT_TQ
    n = q.shape[1]
    nblk = n // batch // ATT_TQ
    nstep = nblk // ATT_SUB
    wide = lambda b, i: b * nstep + i
    prev = lambda b, i: b * nblk + jnp.maximum(ATT_SUB * i - 1, 0)
    nxt = lambda b, i: b * nblk + jnp.minimum(ATT_SUB * (i + 1), nblk - 1)
    rows = lambda f, w: pl.BlockSpec((HEAD_PAIRS, w * ATT_TQ, V7X_LANES), lambda b, i: (0, f(b, i), 0))
    cols = lambda f, w: pl.BlockSpec((NA_WIDTH, w * ATT_TQ), lambda b, i: (0, f(b, i)))
    comb = (masks.shape[0],) + bias.shape[1:]
    vmem = ((int(np.prod(comb)) + bias[0].size + masks.size) * 4 + 16 * (ATT_SUB + 3) * ATT_TQ * NA_WIDTH * 2
            + (NA_HEADS * 4 + 16) * ATT_TQ * ATT_TQ * 4)
    return pl.pallas_call(
        _na_kernel,
        out_shape=jax.ShapeDtypeStruct((HEAD_PAIRS, n, V7X_LANES), BF16),
        grid=(batch, nstep),
        in_specs=[rows(wide, ATT_SUB), cols(prev, 1), cols(wide, ATT_SUB), cols(nxt, 1),
                  rows(prev, 1), rows(wide, ATT_SUB), rows(nxt, 1),
                  cols(lambda b, i: b, 1), rows(lambda b, i: b, 1),
                  _resident(bias.shape[1:], (layer,)), _resident(masks.shape)],
        out_specs=rows(wide, ATT_SUB),
        scratch_shapes=[pltpu.VMEM(comb, F32), pltpu.VMEM((NA_HEADS, ATT_TQ, 4 * ATT_TQ), F32)],
        compiler_params=pltpu.CompilerParams(
            dimension_semantics=("arbitrary", "arbitrary"), vmem_limit_bytes=_vmem_limit(vmem)),
        name="neighbourhood_attention",
    )(q, kt, kt, kt, v, v, v, ktx, vx, bias, masks)


def _ctx_attn_kernel(q_ref, kt_ref, v_ref, o_ref):
    for p in range(HEAD_PAIRS):
        outs = []
        for hh in range(2):
            lanes = slice(hh * HEAD_DIM, (hh + 1) * HEAD_DIM)
            s = _dot(q_ref[p][:, lanes], kt_ref[pl.ds(p * V7X_LANES + hh * HEAD_DIM, HEAD_DIM), :])
            outs.append(_softmax_pv([s], [v_ref[p][:, lanes]]))
        o_ref[p] = jnp.concatenate(outs, axis=1).astype(o_ref.dtype)


def _context_attention(q, kt, v, batch):
    n = q.shape[1]
    spec = pl.BlockSpec((HEAD_PAIRS, n // batch, V7X_LANES), lambda b: (0, b, 0))
    return pl.pallas_call(
        _ctx_attn_kernel,
        out_shape=jax.ShapeDtypeStruct((HEAD_PAIRS, n, V7X_LANES), BF16),
        grid=(batch,),
        in_specs=[spec, pl.BlockSpec((NA_WIDTH, n // batch), lambda b: (0, b)), spec],
        out_specs=spec,
        compiler_params=pltpu.CompilerParams(dimension_semantics=("arbitrary",)),
        name="context_attention",
    )(q, kt, v)


def _rope_tables(seq):
    t = np.arange(seq)
    inv = ROPE_THETA ** (-np.arange(ROPE_PAIRS, dtype=np.float64) / ROPE_PAIRS)
    ang_r = (t // GRID_W)[:, None] * inv
    ang_c = (t % GRID_W)[:, None] * inv
    ang = np.concatenate([ang_r, ang_r, ang_c, ang_c], axis=1)
    first = np.tile(np.arange(HEAD_DIM) % (2 * ROPE_PAIRS) < ROPE_PAIRS, 2)
    cos = np.tile(np.cos(ang), (1, 2))
    sin = np.tile(np.sin(ang), (1, 2))
    sa = np.where(first, -sin, 0.0)
    sb = np.where(first, 0.0, sin)
    return tuple(jnp.asarray(a, F32) for a in (cos, sa, sb))


def _identity_rope_tables(n):
    ones = jnp.ones((n, V7X_LANES), F32)
    zeros = jnp.zeros((n, V7X_LANES), F32)
    return ones, zeros, zeros


def _pool_tables(tm, tb, length):
    assert tb % length == 0 and tm % tb == 0
    t = np.arange(tm)
    pos = t % length
    base = t - pos
    bands, icnts = [], []
    for w in POOL_WINDOWS:
        lo = np.clip(pos - w // 2, 0, length)
        hi = np.clip(pos - w // 2 + w, 0, length)
        s = t[None, :tb]
        bands.append((s >= (base + lo)[:tb, None]) & (s < (base + hi)[:tb, None]))
        icnts.append(np.repeat((1.0 / (hi - lo))[:, None], POOL_CH, axis=1))
    band = jnp.asarray(np.stack(bands), BF16)
    icnt = jnp.asarray(np.concatenate(icnts, axis=1), F32)
    return band, icnt


def _attention_bias(rpb, rows):
    kh = min(NA_KH, rows)
    assert kh == NA_KH and ATT_ROWS == NA_KH // 2 and rows >= 3 * ATT_ROWS
    rpb = rpb * LOG2_E
    by_row = jnp.stack([rpb[:, :, ATT_ROWS - 1 - i:ATT_ROWS - 1 - i + 3 * ATT_ROWS] for i in range(ATT_ROWS)], axis=2)
    pad = GRID_W - NA_KW
    padded = jnp.pad(by_row, ((0, 0),) * 4 + ((pad, pad),))
    toep = jnp.stack([padded[..., GRID_W - 1 - j:2 * GRID_W - 1 - j] for j in range(GRID_W)], axis=4)
    base = jnp.transpose(toep, (0, 1, 2, 4, 3, 5)).reshape(rpb.shape[:2] + (ATT_TQ, 3 * ATT_TQ))

    i = np.arange(ATT_TQ) // GRID_W
    j = np.arange(ATT_TQ) % GRID_W
    m = np.arange(3 * ATT_TQ) // GRID_W
    c = np.arange(3 * ATT_TQ) % GRID_W
    col_start = np.clip(j - NA_KW // 2, 0, GRID_W - NA_KW)
    col_ok = (c[None, :] >= col_start[:, None]) & (c[None, :] < col_start[:, None] + NA_KW)
    masks = []
    for r0 in (0, ATT_ROWS, rows - ATT_ROWS):
        r = r0 + i
        kr = r0 - ATT_ROWS + m
        row_start = np.clip(r - kh // 2, 0, rows - kh)
        row_ok = (kr[None, :] >= row_start[:, None]) & (kr[None, :] < row_start[:, None] + kh)
        masks.append(np.where(row_ok & col_ok, 0.0, NEG_INF))
    return base.astype(F32), jnp.asarray(np.stack(masks), F32)


def kernel(x, c, ctx, c_ctx, w_mod, b_mod, norm_g, w_ffn_gate_up, w_ffn_down, w_in, w_out, na_rpb, w_pool, pool_scale):
    batch, seq, d = x.shape
    ctx_len = ctx.shape[1]
    depth = w_mod.shape[0]
    rows = seq // GRID_W
    tm_x, tm_c = 512, ctx_len

    cv = jnp.concatenate([c, c_ctx[None], jnp.zeros((V7X_SUBLANES - batch - 1, d), F32)])
    mods = _mod_vectors(cv, w_mod, b_mod).reshape(depth, V7X_SUBLANES, N_MOD, 1, d)

    wgu, wdn = w_ffn_gate_up.astype(BF16), w_ffn_down.astype(BF16)
    w_in_b, w_out_b, w_pool_b = w_in.astype(BF16), w_out.astype(BF16), w_pool.astype(BF16)
    norm = norm_g.reshape(depth, norm_g.shape[1], 1, d)
    ps = pool_scale.reshape(depth, 1, POOL_WIDTH)
    bias, masks = _attention_bias(na_rpb, rows)

    rope_x = _rope_tables(seq)
    rope_c = _identity_rope_tables(tm_c)
    band_x, icnt_x = _pool_tables(tm_x, 2 * GRID_W, GRID_W)
    band_c, icnt_c = _pool_tables(tm_c, ctx_len, ctx_len)

    xt = x.reshape(batch * seq, d)
    ct = ctx.reshape(batch * ctx_len, d)
    for l in range(depth):
        last = l == depth - 1
        cx = _Cond(mods, norm, l, 0, batch)
        cc = _Cond(mods, norm, l, batch, 1)
        xt, qx, ktx, vx, px = _ffn(xt, cx, wgu, wdn, 0, tm_x, proj=(w_in_b, rope_x, band_x, icnt_x, w_pool_b, ps))
        ct, qc, ktc, vc, pc = _ffn(ct, cc, wgu, wdn, 0, tm_c, proj=(w_in_b, rope_c, band_c, icnt_c, w_pool_b, ps))
        ax = _neighbourhood_attention(qx, ktx, vx, ktc, vc, bias, masks, l, batch, ctx_len)
        xt = _ffn(xt, cx, wgu, wdn, 1, tm_x, mix=(ax, px, w_out_b))
        if not last:
            ac = _context_attention(qc, ktc, vc, batch)
            ct = _ffn(ct, cc, wgu, wdn, 1, tm_c, mix=(ac, pc, w_out_b))
    return xt.reshape(batch, seq, d)
```

```python
import functools

import numpy as np
import jax
import jax.numpy as jnp
from jax import lax
from jax.experimental import pallas as pl
from jax.experimental.pallas import tpu as pltpu

GRID_W = 64
N_MOD = 9
NA_HEADS = 8
HEAD_DIM = 64
NA_WIDTH = NA_HEADS * HEAD_DIM
NA_KH = 8
NA_KW = 16
POOL_GROUPS = 4
POOL_CH = 128
POOL_WIDTH = POOL_GROUPS * POOL_CH
POOL_WINDOWS = (2, 4, 8, 16)
ROPE_THETA = 10000.0
ROPE_PAIRS = HEAD_DIM // 4
RMS_EPS = 1e-6
NEG_INF = -1e30
LOG2_E = 1.4426950408889634

V7X_LANES = 128
V7X_SUBLANES = 8
V7X_VMEM_BYTES = 64 * 1024 * 1024
V7X_MXU_DIM = 256
COMPILER_SCRATCH_BYTES = 8 << 20
VMEM_RESERVE_BYTES = 6 << 20

HEAD_PAIRS = NA_WIDTH // V7X_LANES
FF_CHUNK = V7X_MXU_DIM
ATT_ROWS = NA_KH // 2
ATT_TQ = ATT_ROWS * GRID_W

F32 = jnp.float32
BF16 = jnp.bfloat16


def _vmem_limit(estimate_bytes):
    return int(min(estimate_bytes + COMPILER_SCRATCH_BYTES, V7X_VMEM_BYTES - VMEM_RESERVE_BYTES))


def _resident(block_shape, lead=()):
    index = tuple(lead) + (0,) * len(block_shape)
    return pl.BlockSpec((None,) * len(lead) + tuple(block_shape), lambda *_: index, pipeline_mode=pl.Buffered(1))


def _rms(x, g):
    return x * lax.rsqrt(jnp.mean(x * x, axis=-1, keepdims=True) + RMS_EPS) * g


def _dot(a, b):
    return jnp.dot(a, b, preferred_element_type=F32)


def _split_bf16(x):
    hi = x.astype(BF16)
    lo = (x - hi.astype(F32)).astype(BF16)
    return hi, lo


def _mod_kernel(cv_ref, w_ref, b_ref, o_ref):
    s = cv_ref[...]
    s = s * jax.nn.sigmoid(s)
    s_hi, s_lo = _split_bf16(s)
    w_hi, w_lo = _split_bf16(w_ref[...])
    n = s.shape[0]
    r = _dot(jnp.concatenate([s_hi, s_lo], axis=0), w_hi)
    o_ref[...] = r[:n] + r[n:] + _dot(s_hi, w_lo) + b_ref[...]


def _mod_vectors(cv, w_mod, b_mod):
    depth, d, n = w_mod.shape
    tn = 1536
    return pl.pallas_call(
        _mod_kernel,
        out_shape=jax.ShapeDtypeStruct((depth, cv.shape[0], n), F32),
        grid=(depth, n // tn),
        in_specs=[
            pl.BlockSpec(cv.shape, lambda l, j: (0, 0)),
            pl.BlockSpec((None, d, tn), lambda l, j: (l, 0, j)),
            pl.BlockSpec((None, 1, tn), lambda l, j: (l, 0, j)),
        ],
        out_specs=pl.BlockSpec((None, cv.shape[0], tn), lambda l, j: (l, 0, j)),
        compiler_params=pltpu.CompilerParams(
            dimension_semantics=("arbitrary", "arbitrary"),
            vmem_limit_bytes=_vmem_limit(5 * d * tn * 4)),
        name="mod_vectors",
    )(cv, w_mod, b_mod.reshape(depth, 1, n))


class _Cond:
    def __init__(self, mods, norm, layer, row0, n_rows):
        self.mods, self.norm, self.layer, self.row0, self.n_rows = mods, norm, layer, row0, n_rows

    def specs(self, n_tiles):
        per_row = n_tiles // self.n_rows
        layer, row0 = self.layer, self.row0
        return [pl.BlockSpec((None, None) + self.mods.shape[2:], lambda i: (layer, row0 + i // per_row, 0, 0, 0)),
                pl.BlockSpec((None,) + self.norm.shape[1:], lambda i: (layer, 0, 0, 0))]


def _ffn_kernel(*refs, mixed, projected, m0, g0):
    refs = list(refs)
    x_ref = refs.pop(0)
    mix_refs = [refs.pop(0) for _ in range(3)] if mixed else None
    mod_all, g_all, wgu_ref, wd_ref = (refs.pop(0) for _ in range(4))
    proj_in = [refs.pop(0) for _ in range(8)] if projected else None
    o_ref = refs.pop(0)
    proj_out = [refs.pop(0) for _ in range(4)] if projected else None
    h_scr, acc_scr = refs
    if mixed:
        a_ref, p_ref, wo_ref = mix_refs
        att = jnp.concatenate([a_ref[c] for c in range(a_ref.shape[0])], axis=1)
        y = _dot(att, wo_ref[pl.ds(0, NA_WIDTH), :]) + _dot(p_ref[...], wo_ref[pl.ds(NA_WIDTH, POOL_WIDTH), :])
        o_ref[...] = x_ref[...] + _rms(y, mod_all[m0] * g_all[g0])
        x_ref, m0, g0 = o_ref, m0 + 1, g0 + 1
    mod_ref, g_ref = mod_all.at[pl.ds(m0, 3)], g_all.at[pl.ds(g0, 2)]
    h = _rms(x_ref[...], g_ref[0] * (1.0 + mod_ref[1])) + mod_ref[0]
    h_scr[...] = h.astype(BF16)
    d_ff = wd_ref.shape[0]
    for j in range(d_ff // FF_CHUNK):
        hb = h_scr[...]
        gate = _dot(hb, wgu_ref[:, pl.ds(j * FF_CHUNK, FF_CHUNK)])
        up = _dot(hb, wgu_ref[:, pl.ds(d_ff + j * FF_CHUNK, FF_CHUNK)])
        act = (gate * jax.nn.sigmoid(gate) * up).astype(BF16)
        part = _dot(act, wd_ref[pl.ds(j * FF_CHUNK, FF_CHUNK), :])
        if j == 0:
            acc_scr[...] = part
        else:
            acc_scr[...] += part
    o_ref[...] = x_ref[...] + _rms(acc_scr[...], 0.5 * mod_ref[2] * g_ref[1])
    if projected:
        _token_mix_inputs(o_ref, mod_all, g_all, *proj_in, *proj_out)


def _ffn(xt, cond, wgu, wd, half, tm, mix=None, proj=None):
    n, d = xt.shape
    nb = n // tm
    layer = (cond.layer,)
    which = (cond.layer, half)
    d_ff = wd.shape[2]
    assert d_ff % FF_CHUNK == 0 and wgu.shape[2:] == (d, 2 * d_ff)
    tok = pl.BlockSpec((tm, d), lambda i: (i, 0))
    vmem = 4 * tm * d * 4 + tm * d * 2 + tm * d * 4 + 3 * d * d_ff * 2 + 4 * tm * FF_CHUNK * 4
    operands, specs = [xt], [tok]
    if mix is not None:
        att, pool, w_out = mix
        operands += [att, pool, w_out]
        specs += [pl.BlockSpec((att.shape[0], tm, att.shape[2]), lambda i: (0, i, 0)),
                  pl.BlockSpec((tm, pool.shape[1]), lambda i: (i, 0)), _resident(w_out.shape[1:], (cond.layer,))]
        vmem += 4 * tm * pool.shape[1] * 2 + 2 * d * d * 2 + 2 * tm * d * 4
    operands += [cond.mods, cond.norm, wgu, wd]
    specs += cond.specs(nb) + [_resident(wgu.shape[2:], which), _resident(wd.shape[2:], which)]
    out_shape, out_specs = [jax.ShapeDtypeStruct((n, d), F32)], [tok]
    if proj is not None:
        w_in, rope_tabs, band, icnt, w_pool, pool_scale = proj
        tab_blocks = rope_tabs[0].shape[0] // tm
        tab_spec = pl.BlockSpec((tm, V7X_LANES), lambda i: (i % tab_blocks, 0))
        operands += [w_in, *rope_tabs, band, icnt, w_pool, pool_scale]
        specs += [_resident(w_in.shape[1:], layer), tab_spec, tab_spec, tab_spec, _resident(band.shape),
                  _resident(icnt.shape), _resident(w_pool.shape[1:], layer), _resident(pool_scale.shape[1:], layer)]
        pair = jax.ShapeDtypeStruct((HEAD_PAIRS, n, V7X_LANES), BF16)
        pair_spec = pl.BlockSpec((HEAD_PAIRS, tm, V7X_LANES), lambda i: (0, i, 0))
        out_shape += [pair, jax.ShapeDtypeStruct((NA_WIDTH, n), BF16), pair, jax.ShapeDtypeStruct((n, POOL_WIDTH), BF16)]
        out_specs += [pair_spec, pl.BlockSpec((NA_WIDTH, tm), lambda i: (0, i)), pair_spec,
                      pl.BlockSpec((tm, POOL_WIDTH), lambda i: (i, 0))]
        vmem += (w_in[0].size * 2 + 6 * tm * V7X_LANES * 4 + band.size * 2 + icnt.size * 4
                 + 8 * tm * NA_WIDTH * 2 + 6 * tm * NA_WIDTH * 4)
    m0, g0 = (5, 3) if mix is not None else (3 * 2 * half, 4 * half)
    outs = pl.pallas_call(
        functools.partial(_ffn_kernel, mixed=mix is not None, projected=proj is not None, m0=m0, g0=g0),
        out_shape=out_shape,
        grid=(nb,),
        in_specs=specs,
        out_specs=out_specs,
        scratch_shapes=[pltpu.VMEM((tm, d), BF16), pltpu.VMEM((tm, d), F32)],
        compiler_params=pltpu.CompilerParams(
            dimension_semantics=("arbitrary",), vmem_limit_bytes=_vmem_limit(vmem)),
        name="mix_ffn" if mix is not None else ("ffn_proj" if proj is not None else "sandwich_ffn"),
    )(*operands)
    return outs if proj is not None else outs[0]


def _token_mix_inputs(x_ref, mod_ref, g_ref, w_ref, cos_ref, sa_ref, sb_ref, band_ref, icnt_ref, wp_ref, ps_ref,
                      q_ref, kt_ref, v_ref, p_ref):
    h = (_rms(x_ref[...], g_ref[2] * (1.0 + mod_ref[4])) + mod_ref[3]).astype(BF16)
    cos, sa, sb = cos_ref[...], sa_ref[...], sb_ref[...]

    def rope(t):
        up = pltpu.roll(t, V7X_LANES - ROPE_PAIRS, 1)
        down = pltpu.roll(t, ROPE_PAIRS, 1)
        return t * cos + up * sa + down * sb

    scale = HEAD_DIM ** -0.5 * LOG2_E
    qk =_dot(h, w_ref[:, pl.ds(0, 2 * NA_WIDTH)])
    for c in range(NA_WIDTH // V7X_LANES):
        lanes = pl.ds(c * V7X_LANES, V7X_LANES)
        q_ref[c] = (rope(qk[:, c * V7X_LANES:(c + 1) * V7X_LANES]) * scale).astype(BF16)
        k_rot = rope(qk[:, NA_WIDTH + c * V7X_LANES:NA_WIDTH + (c + 1) * V7X_LANES])
        kt_ref[lanes, :] = k_rot.T.astype(BF16)
    vu = _dot(h, w_ref[:, pl.ds(2 * NA_WIDTH, NA_WIDTH + POOL_WIDTH)])
    for c in range(NA_WIDTH // V7X_LANES):
        v_ref[c] = vu[:, c * V7X_LANES:(c + 1) * V7X_LANES].astype(BF16)
    tm, tb = x_ref.shape[0], band_ref.shape[1]
    for g in range(POOL_GROUPS):
        lanes = pl.ds(g * POOL_CH, POOL_CH)
        u = vu[:, NA_WIDTH + g * POOL_CH:NA_WIDTH + (g + 1) * POOL_CH]
        u2 = jnp.concatenate(_split_bf16(u), axis=1)
        band = band_ref[g]
        win2 = jnp.concatenate([_dot(band, u2[s * tb:(s + 1) * tb]) for s in range(tm // tb)], axis=0)
        win = win2[:, :POOL_CH] + win2[:, POOL_CH:]
        dlt = (win * icnt_ref[:, lanes] - u).astype(BF16)
        p_ref[:, lanes] = (_dot(dlt, wp_ref[g]) * ps_ref[:, lanes]).astype(BF16)


def _softmax_pv(s, v):
    e = jnp.exp2(s - jnp.max(s, axis=-1, keepdims=True))
    return _dot(e.astype(BF16), v) / jnp.sum(e, axis=-1, keepdims=True)


ATT_SUB = 2


def _na_kernel(q_ref, kp_ref, km_ref, kn_ref, vp_ref, vm_ref, vn_ref, kx_ref, vx_ref, bias_ref, mask_ref, o_ref,
               comb_scr, s_scr):
    i, last = pl.program_id(1), pl.num_programs(1) - 1

    @pl.when((pl.program_id(0) == 0) & (i == 0))
    def _():
        for var in range(comb_scr.shape[0]):
            for h in range(NA_HEADS):
                comb_scr[var, h] = bias_ref[h] + mask_ref[var]

    n_loc = 3 * ATT_TQ
    mids = [slice(max(sub - 1, 0) * ATT_TQ, min(sub + 2, ATT_SUB) * ATT_TQ) for sub in range(ATT_SUB)]
    for sub in range(ATT_SUB):
        variant = 1
        if sub == 0:
            variant = jnp.where(i == 0, 0, variant)
        if sub == ATT_SUB - 1:
            variant = jnp.where(i == last, 2, variant)
        for h in range(NA_HEADS):
            p, lanes = h // 2, slice((h % 2) * HEAD_DIM, (h % 2 + 1) * HEAD_DIM)
            feat = pl.ds(h * HEAD_DIM, HEAD_DIM)
            kts = ([kp_ref[feat, :]] if sub == 0 else []) + [km_ref[feat, mids[sub]]]
            kts += ([kn_ref[feat, :]] if sub == ATT_SUB - 1 else []) + [kx_ref[feat, :]]
            s = _dot(q_ref[p][sub * ATT_TQ:(sub + 1) * ATT_TQ, lanes], jnp.concatenate(kts, axis=1))
            s_scr[h] = jnp.concatenate([s[:, :n_loc] + comb_scr[variant, h], s[:, n_loc:]], axis=1)
        for p in range(HEAD_PAIRS):
            outs = []
            for hh in range(2):
                lanes = slice(hh * HEAD_DIM, (hh + 1) * HEAD_DIM)
                vals = ([vp_ref[p][:, lanes]] if sub == 0 else []) + [vm_ref[p][mids[sub], lanes]]
                vals += ([vn_ref[p][:, lanes]] if sub == ATT_SUB - 1 else []) + [vx_ref[p][:, lanes]]
                outs.append(_softmax_pv(s_scr[2 * p + hh], jnp.concatenate(vals, axis=0)))
            o_ref[p, sub * ATT_TQ:(sub + 1) * ATT_TQ, :] = jnp.concatenate(outs, axis=1).astype(o_ref.dtype)


def _neighbourhood_attention(q, kt, v, ktx, vx, bias, masks, layer, batch, ctx_len):
    assert ctx_len == ATT_TQ
    n = q.shape[1]
    nblk = n // batch // ATT_TQ
    nstep = nblk // ATT_SUB
    wide = lambda b, i: b * nstep + i
    prev = lambda b, i: b * nblk + jnp.maximum(ATT_SUB * i - 1, 0)
    nxt = lambda b, i: b * nblk + jnp.minimum(ATT_SUB * (i + 1), nblk - 1)
    rows = lambda f, w: pl.BlockSpec((HEAD_PAIRS, w * ATT_TQ, V7X_LANES), lambda b, i: (0, f(b, i), 0))
    cols = lambda f, w: pl.BlockSpec((NA_WIDTH, w * ATT_TQ), lambda b, i: (0, f(b, i)))
    comb = (masks.shape[0],) + bias.shape[1:]
    vmem = ((int(np.prod(comb)) + bias[0].size + masks.size) * 4 + 16 * (ATT_SUB + 3) * ATT_TQ * NA_WIDTH * 2
            + (NA_HEADS * 4 + 16) * ATT_TQ * ATT_TQ * 4)
    return pl.pallas_call(
        _na_kernel,
        out_shape=jax.ShapeDtypeStruct((HEAD_PAIRS, n, V7X_LANES), BF16),
        grid=(batch, nstep),
        in_specs=[rows(wide, ATT_SUB), cols(prev, 1), cols(wide, ATT_SUB), cols(nxt, 1),
                  rows(prev, 1), rows(wide, ATT_SUB), rows(nxt, 1),
                  cols(lambda b, i: b, 1), rows(lambda b, i: b, 1),
                  _resident(bias.shape[1:], (layer,)), _resident(masks.shape)],
        out_specs=rows(wide, ATT_SUB),
        scratch_shapes=[pltpu.VMEM(comb, F32), pltpu.VMEM((NA_HEADS, ATT_TQ, 4 * ATT_TQ), F32)],
        compiler_params=pltpu.CompilerParams(
            dimension_semantics=("arbitrary", "arbitrary"), vmem_limit_bytes=_vmem_limit(vmem)),
        name="neighbourhood_attention",
    )(q, kt, kt, kt, v, v, v, ktx, vx, bias, masks)


def _ctx_attn_kernel(q_ref, kt_ref, v_ref, o_ref):
    for p in range(HEAD_PAIRS):
        outs = []
        for hh in range(2):
            lanes = slice(hh * HEAD_DIM, (hh + 1) * HEAD_DIM)
            s = _dot(q_ref[p][:, lanes], kt_ref[pl.ds(p * V7X_LANES + hh * HEAD_DIM, HEAD_DIM), :])
            outs.append(_softmax_pv(s, v_ref[p][:, lanes]))
        o_ref[p] = jnp.concatenate(outs, axis=1).astype(o_ref.dtype)


def _context_attention(q, kt, v, batch):
    n = q.shape[1]
    spec = pl.BlockSpec((HEAD_PAIRS, n // batch, V7X_LANES), lambda b: (0, b, 0))
    return pl.pallas_call(
        _ctx_attn_kernel,
        out_shape=jax.ShapeDtypeStruct((HEAD_PAIRS, n, V7X_LANES), BF16),
        grid=(batch,),
        in_specs=[spec, pl.BlockSpec((NA_WIDTH, n // batch), lambda b: (0, b)), spec],
        out_specs=spec,
        compiler_params=pltpu.CompilerParams(dimension_semantics=("arbitrary",)),
        name="context_attention",
    )(q, kt, v)


def _rope_tables(seq):
    t = np.arange(seq)
    inv = ROPE_THETA ** (-np.arange(ROPE_PAIRS, dtype=np.float64) / ROPE_PAIRS)
    ang_r = (t // GRID_W)[:, None] * inv
    ang_c = (t % GRID_W)[:, None] * inv
    ang = np.concatenate([ang_r, ang_r, ang_c, ang_c], axis=1)
    first = np.tile(np.arange(HEAD_DIM) % (2 * ROPE_PAIRS) < ROPE_PAIRS, 2)
    cos = np.tile(np.cos(ang), (1, 2))
    sin = np.tile(np.sin(ang), (1, 2))
    sa = np.where(first, -sin, 0.0)
    sb = np.where(first, 0.0, sin)
    return tuple(jnp.asarray(a, F32) for a in (cos, sa, sb))


def _identity_rope_tables(n):
    ones = jnp.ones((n, V7X_LANES), F32)
    zeros = jnp.zeros((n, V7X_LANES), F32)
    return ones, zeros, zeros


def _pool_tables(tm, tb, length):
    assert tb % length == 0 and tm % tb == 0
    t = np.arange(tm)
    pos = t % length
    base = t - pos
    bands, icnts = [], []
    for w in POOL_WINDOWS:
        lo = np.clip(pos - w // 2, 0, length)
        hi = np.clip(pos - w // 2 + w, 0, length)
        s = t[None, :tb]
        bands.append((s >= (base + lo)[:tb, None]) & (s < (base + hi)[:tb, None]))
        icnts.append(np.repeat((1.0 / (hi - lo))[:, None], POOL_CH, axis=1))
    band = jnp.asarray(np.stack(bands), BF16)
    icnt = jnp.asarray(np.concatenate(icnts, axis=1), F32)
    return band, icnt


def _attention_bias(rpb, rows):
    kh = min(NA_KH, rows)
    assert kh == NA_KH and ATT_ROWS == NA_KH // 2 and rows >= 3 * ATT_ROWS
    rpb = rpb * LOG2_E
    by_row = jnp.stack([rpb[:, :, ATT_ROWS - 1 - i:ATT_ROWS - 1 - i + 3 * ATT_ROWS] for i in range(ATT_ROWS)], axis=2)
    pad = GRID_W - NA_KW
    padded = jnp.pad(by_row, ((0, 0),) * 4 + ((pad, pad),))
    toep = jnp.stack([padded[..., GRID_W - 1 - j:2 * GRID_W - 1 - j] for j in range(GRID_W)], axis=3)
    base = toep.reshape(rpb.shape[:2] + (ATT_TQ, 3 * ATT_TQ))

    i = np.arange(ATT_TQ) // GRID_W
    j = np.arange(ATT_TQ) % GRID_W
    m = np.arange(3 * ATT_TQ) // GRID_W
    c = np.arange(3 * ATT_TQ) % GRID_W
    col_start = np.clip(j - NA_KW // 2, 0, GRID_W - NA_KW)
    col_ok = (c[None, :] >= col_start[:, None]) & (c[None, :] < col_start[:, None] + NA_KW)
    masks = []
    for r0 in (0, ATT_ROWS, rows - ATT_ROWS):
        r = r0 + i
        kr = r0 - ATT_ROWS + m
        row_start = np.clip(r - kh // 2, 0, rows - kh)
        row_ok = (kr[None, :] >= row_start[:, None]) & (kr[None, :] < row_start[:, None] + kh)
        masks.append(np.where(row_ok & col_ok, 0.0, NEG_INF))
    return base.astype(F32), jnp.asarray(np.stack(masks), F32)


def kernel(x, c, ctx, c_ctx, w_mod, b_mod, norm_g, w_ffn_gate_up, w_ffn_down, w_in, w_out, na_rpb, w_pool, pool_scale):
    batch, seq, d = x.shape
    ctx_len = ctx.shape[1]
    depth = w_mod.shape[0]
    rows = seq // GRID_W
    tm_x, tm_c = 512, batch * ctx_len

    cv = jnp.concatenate([c, c_ctx[None], jnp.zeros((V7X_SUBLANES - batch - 1, d), F32)])
    mods = _mod_vectors(cv, w_mod, b_mod).reshape(depth, V7X_SUBLANES, N_MOD, 1, d)

    wgu, wdn = w_ffn_gate_up.astype(BF16), w_ffn_down.astype(BF16)
    w_in_b, w_out_b, w_pool_b = w_in.astype(BF16), w_out.astype(BF16), w_pool.astype(BF16)
    norm = norm_g.reshape(depth, norm_g.shape[1], 1, d)
    ps = pool_scale.reshape(depth, 1, POOL_WIDTH)
    bias, masks = _attention_bias(na_rpb, rows)

    rope_x = _rope_tables(seq)
    rope_c = _identity_rope_tables(tm_c)
    band_x, icnt_x = _pool_tables(tm_x, 2 * GRID_W, GRID_W)
    band_c, icnt_c = _pool_tables(tm_c, ctx_len, ctx_len)

    xt = x.reshape(batch * seq, d)
    ct = ctx.reshape(batch * ctx_len, d)
    for l in range(depth):
        last = l == depth - 1
        cx = _Cond(mods, norm, l, 0, batch)
        cc = _Cond(mods, norm, l, batch, 1)
        xt, qx, ktx, vx, px = _ffn(xt, cx, wgu, wdn, 0, tm_x, proj=(w_in_b, rope_x, band_x, icnt_x, w_pool_b, ps))
        ct, qc, ktc, vc, pc = _ffn(ct, cc, wgu, wdn, 0, tm_c, proj=(w_in_b, rope_c, band_c, icnt_c, w_pool_b, ps))
        ax = _neighbourhood_attention(qx, ktx, vx, ktc, vc, bias, masks, l, batch, ctx_len)
        xt = _ffn(xt, cx, wgu, wdn, 1, tm_x, mix=(ax, px, w_out_b))
        if not last:
            ac = _context_attention(qc, ktc, vc, batch)
            ct = _ffn(ct, cc, wgu, wdn, 1, tm_c, mix=(ac, pc, w_out_b))
    return xt.reshape(batch, seq, d)
```

```python
import functools

import numpy as np
import jax
import jax.numpy as jnp
from jax import lax
from jax.experimental import pallas as pl
from jax.experimental.pallas import tpu as pltpu

GRID_W = 64
N_MOD = 9
NA_HEADS = 8
HEAD_DIM = 64
NA_WIDTH = NA_HEADS * HEAD_DIM
NA_KH = 8
NA_KW = 16
POOL_GROUPS = 4
POOL_CH = 128
POOL_WIDTH = POOL_GROUPS * POOL_CH
POOL_WINDOWS = (2, 4, 8, 16)
ROPE_THETA = 10000.0
ROPE_PAIRS = HEAD_DIM // 4
RMS_EPS = 1e-6
NEG_INF = -1e30
LOG2_E = 1.4426950408889634

V7X_LANES = 128
V7X_SUBLANES = 8
V7X_VMEM_BYTES = 64 * 1024 * 1024
V7X_MXU_DIM = 256
COMPILER_SCRATCH_BYTES = 8 << 20
VMEM_RESERVE_BYTES = 6 << 20

HEAD_PAIRS = NA_WIDTH // V7X_LANES
FF_CHUNK = V7X_MXU_DIM
ATT_ROWS = NA_KH // 2
ATT_TQ = ATT_ROWS * GRID_W

F32 = jnp.float32
BF16 = jnp.bfloat16


def _vmem_limit(estimate_bytes):
    return int(min(estimate_bytes + COMPILER_SCRATCH_BYTES, V7X_VMEM_BYTES - VMEM_RESERVE_BYTES))


def _resident(block_shape, lead=()):
    index = tuple(lead) + (0,) * len(block_shape)
    return pl.BlockSpec((None,) * len(lead) + tuple(block_shape), lambda *_: index, pipeline_mode=pl.Buffered(1))


def _rms(x, g):
    return x * lax.rsqrt(jnp.mean(x * x, axis=-1, keepdims=True) + RMS_EPS) * g


def _dot(a, b):
    return jnp.dot(a, b, preferred_element_type=F32)


def _split_bf16(x):
    hi = x.astype(BF16)
    lo = (x - hi.astype(F32)).astype(BF16)
    return hi, lo


def _mod_kernel(cv_ref, w_ref, b_ref, o_ref):
    s = cv_ref[...]
    s = s * jax.nn.sigmoid(s)
    s_hi, s_lo = _split_bf16(s)
    w_hi, w_lo = _split_bf16(w_ref[...])
    n = s.shape[0]
    r = _dot(jnp.concatenate([s_hi, s_lo], axis=0), w_hi)
    o_ref[...] = r[:n] + r[n:] + _dot(s_hi, w_lo) + b_ref[...]


def _mod_vectors(cv, w_mod, b_mod):
    depth, d, n = w_mod.shape
    tn = 1536
    return pl.pallas_call(
        _mod_kernel,
        out_shape=jax.ShapeDtypeStruct((depth, cv.shape[0], n), F32),
        grid=(depth, n // tn),
        in_specs=[
            pl.BlockSpec(cv.shape, lambda l, j: (0, 0)),
            pl.BlockSpec((None, d, tn), lambda l, j: (l, 0, j)),
            pl.BlockSpec((None, 1, tn), lambda l, j: (l, 0, j)),
        ],
        out_specs=pl.BlockSpec((None, cv.shape[0], tn), lambda l, j: (l, 0, j)),
        compiler_params=pltpu.CompilerParams(
            dimension_semantics=("arbitrary", "arbitrary"),
            vmem_limit_bytes=_vmem_limit(5 * d * tn * 4)),
        name="mod_vectors",
    )(cv, w_mod, b_mod.reshape(depth, 1, n))


class _Cond:
    def __init__(self, mods, norm, layer, row0, n_rows):
        self.mods, self.norm, self.layer, self.row0, self.n_rows = mods, norm, layer, row0, n_rows

    def specs(self, n_tiles):
        per_row = n_tiles // self.n_rows
        layer, row0 = self.layer, self.row0
        return [pl.BlockSpec((None, None) + self.mods.shape[2:], lambda i: (layer, row0 + i // per_row, 0, 0, 0)),
                pl.BlockSpec((None,) + self.norm.shape[1:], lambda i: (layer, 0, 0, 0))]


def _ffn_kernel(*refs, mixed, projected, m0, g0):
    refs = list(refs)
    x_ref = refs.pop(0)
    mix_refs = [refs.pop(0) for _ in range(3)] if mixed else None
    mod_all, g_all, wgu_ref, wd_ref = (refs.pop(0) for _ in range(4))
    proj_in = [refs.pop(0) for _ in range(8)] if projected else None
    o_ref = refs.pop(0)
    proj_out = [refs.pop(0) for _ in range(4)] if projected else None
    h_scr, acc_scr = refs
    if mixed:
        a_ref, p_ref, wo_ref = mix_refs
        att = jnp.concatenate([a_ref[c] for c in range(a_ref.shape[0])], axis=1)
        y = _dot(att, wo_ref[pl.ds(0, NA_WIDTH), :]) + _dot(p_ref[...], wo_ref[pl.ds(NA_WIDTH, POOL_WIDTH), :])
        o_ref[...] = x_ref[...] + _rms(y, mod_all[m0] * g_all[g0])
        x_ref, m0, g0 = o_ref, m0 + 1, g0 + 1
    mod_ref, g_ref = mod_all.at[pl.ds(m0, 3)], g_all.at[pl.ds(g0, 2)]
    h = _rms(x_ref[...], g_ref[0] * (1.0 + mod_ref[1])) + mod_ref[0]
    h_scr[...] = h.astype(BF16)
    d_ff = wd_ref.shape[0]
    for j in range(d_ff // FF_CHUNK):
        hb = h_scr[...]
        gate = _dot(hb, wgu_ref[:, pl.ds(j * FF_CHUNK, FF_CHUNK)])
        up = _dot(hb, wgu_ref[:, pl.ds(d_ff + j * FF_CHUNK, FF_CHUNK)])
        act = (gate * jax.nn.sigmoid(gate) * up).astype(BF16)
        part = _dot(act, wd_ref[pl.ds(j * FF_CHUNK, FF_CHUNK), :])
        if j == 0:
            acc_scr[...] = part
        else:
            acc_scr[...] += part
    o_ref[...] = x_ref[...] + _rms(acc_scr[...], 0.5 * mod_ref[2] * g_ref[1])
    if projected:
        _token_mix_inputs(o_ref, mod_all, g_all, *proj_in, *proj_out)


def _ffn(xt, cond, wgu, wd, half, tm, mix=None, proj=None):
    n, d = xt.shape
    nb = n // tm
    layer = (cond.layer,)
    which = (cond.layer, half)
    d_ff = wd.shape[2]
    assert d_ff % FF_CHUNK == 0 and wgu.shape[2:] == (d, 2 * d_ff)
    tok = pl.BlockSpec((tm, d), lambda i: (i, 0))
    vmem = 4 * tm * d * 4 + tm * d * 2 + tm * d * 4 + 3 * d * d_ff * 2 + 4 * tm * FF_CHUNK * 4
    operands, specs = [xt], [tok]
    if mix is not None:
        att, pool, w_out = mix
        operands += [att, pool, w_out]
        specs += [pl.BlockSpec((att.shape[0], tm, att.shape[2]), lambda i: (0, i, 0)),
                  pl.BlockSpec((tm, pool.shape[1]), lambda i: (i, 0)), _resident(w_out.shape[1:], (cond.layer,))]
        vmem += 4 * tm * pool.shape[1] * 2 + 2 * d * d * 2 + 2 * tm * d * 4
    operands += [cond.mods, cond.norm, wgu, wd]
    specs += cond.specs(nb) + [_resident(wgu.shape[2:], which), _resident(wd.shape[2:], which)]
    out_shape, out_specs = [jax.ShapeDtypeStruct((n, d), F32)], [tok]
    if proj is not None:
        w_in, rope_tabs, band, icnt, w_pool, pool_scale = proj
        tab_blocks = rope_tabs[0].shape[0] // tm
        tab_spec = pl.BlockSpec((tm, V7X_LANES), lambda i: (i % tab_blocks, 0))
        operands += [w_in, *rope_tabs, band, icnt, w_pool, pool_scale]
        specs += [_resident(w_in.shape[1:], layer), tab_spec, tab_spec, tab_spec, _resident(band.shape),
                  _resident(icnt.shape), _resident(w_pool.shape[1:], layer), _resident(pool_scale.shape[1:], layer)]
        pair = jax.ShapeDtypeStruct((HEAD_PAIRS, n, V7X_LANES), BF16)
        pair_spec = pl.BlockSpec((HEAD_PAIRS, tm, V7X_LANES), lambda i: (0, i, 0))
        out_shape += [pair, jax.ShapeDtypeStruct((NA_WIDTH, n), BF16), pair, jax.ShapeDtypeStruct((n, POOL_WIDTH), BF16)]
        out_specs += [pair_spec, pl.BlockSpec((NA_WIDTH, tm), lambda i: (0, i)), pair_spec,
                      pl.BlockSpec((tm, POOL_WIDTH), lambda i: (i, 0))]
        vmem += (w_in[0].size * 2 + 6 * tm * V7X_LANES * 4 + band.size * 2 + icnt.size * 4
                 + 8 * tm * NA_WIDTH * 2 + 6 * tm * NA_WIDTH * 4)
    m0, g0 = (5, 3) if mix is not None else (3 * 2 * half, 4 * half)
    outs = pl.pallas_call(
        functools.partial(_ffn_kernel, mixed=mix is not None, projected=proj is not None, m0=m0, g0=g0),
        out_shape=out_shape,
        grid=(nb,),
        in_specs=specs,
        out_specs=out_specs,
        scratch_shapes=[pltpu.VMEM((tm, d), BF16), pltpu.VMEM((tm, d), F32)],
        compiler_params=pltpu.CompilerParams(
            dimension_semantics=("arbitrary",), vmem_limit_bytes=_vmem_limit(vmem)),
        name="mix_ffn" if mix is not None else ("ffn_proj" if proj is not None else "sandwich_ffn"),
    )(*operands)
    return outs if proj is not None else outs[0]


def _token_mix_inputs(x_ref, mod_ref, g_ref, w_ref, cos_ref, sa_ref, sb_ref, band_ref, icnt_ref, wp_ref, ps_ref,
                      q_ref, kt_ref, v_ref, p_ref):
    h = (_rms(x_ref[...], g_ref[2] * (1.0 + mod_ref[4])) + mod_ref[3]).astype(BF16)
    cos, sa, sb = cos_ref[...], sa_ref[...], sb_ref[...]

    def rope(t):
        up = pltpu.roll(t, V7X_LANES - ROPE_PAIRS, 1)
        down = pltpu.roll(t, ROPE_PAIRS, 1)
        return t * cos + up * sa + down * sb

    scale = HEAD_DIM ** -0.5 * LOG2_E
    qk =_dot(h, w_ref[:, pl.ds(0, 2 * NA_WIDTH)])
    for c in range(NA_WIDTH // V7X_LANES):
        lanes = pl.ds(c * V7X_LANES, V7X_LANES)
        q_ref[c] = (rope(qk[:, c * V7X_LANES:(c + 1) * V7X_LANES]) * scale).astype(BF16)
        k_rot = rope(qk[:, NA_WIDTH + c * V7X_LANES:NA_WIDTH + (c + 1) * V7X_LANES])
        kt_ref[lanes, :] = k_rot.T.astype(BF16)
    vu = _dot(h, w_ref[:, pl.ds(2 * NA_WIDTH, NA_WIDTH + POOL_WIDTH)])
    for c in range(NA_WIDTH // V7X_LANES):
        v_ref[c] = vu[:, c * V7X_LANES:(c + 1) * V7X_LANES].astype(BF16)
    tm, tb = x_ref.shape[0], band_ref.shape[1]
    for g in range(POOL_GROUPS):
        lanes = pl.ds(g * POOL_CH, POOL_CH)
        u = vu[:, NA_WIDTH + g * POOL_CH:NA_WIDTH + (g + 1) * POOL_CH]
        u2 = jnp.concatenate(_split_bf16(u), axis=1)
        band = band_ref[g]
        win2 = jnp.concatenate([_dot(band, u2[s * tb:(s + 1) * tb]) for s in range(tm // tb)], axis=0)
        win = win2[:, :POOL_CH] + win2[:, POOL_CH:]
        dlt = (win * icnt_ref[:, lanes] - u).astype(BF16)
        p_ref[:, lanes] = (_dot(dlt, wp_ref[g]) * ps_ref[:, lanes]).astype(BF16)


def _softmax_pv(s, v):
    e = jnp.exp2(s - jnp.max(s, axis=-1, keepdims=True))
    return _dot(e.astype(BF16), v) / jnp.sum(e, axis=-1, keepdims=True)


ATT_SUB = 2


def _na_kernel(q_ref, kp_ref, km_ref, kn_ref, vp_ref, vm_ref, vn_ref, kx_ref, vx_ref, bias_ref, mask_ref, o_ref,
               comb_scr, s_scr):
    i, last = pl.program_id(1), pl.num_programs(1) - 1

    @pl.when((pl.program_id(0) == 0) & (i == 0))
    def _():
        for var in range(comb_scr.shape[0]):
            for h in range(NA_HEADS):
                comb_scr[var, h] = bias_ref[h] + mask_ref[var]

    n_loc = 3 * ATT_TQ
    mids = [slice(max(sub - 1, 0) * ATT_TQ, min(sub + 2, ATT_SUB) * ATT_TQ) for sub in range(ATT_SUB)]
    for sub in range(ATT_SUB):
        variant = 1
        if sub == 0:
            variant = jnp.where(i == 0, 0, variant)
        if sub == ATT_SUB - 1:
            variant = jnp.where(i == last, 2, variant)
        for h in range(NA_HEADS):
            p, lanes = h // 2, slice((h % 2) * HEAD_DIM, (h % 2 + 1) * HEAD_DIM)
            feat = pl.ds(h * HEAD_DIM, HEAD_DIM)
            kts = ([kp_ref[feat, :]] if sub == 0 else []) + [km_ref[feat, mids[sub]]]
            kts += ([kn_ref[feat, :]] if sub == ATT_SUB - 1 else []) + [kx_ref[feat, :]]
            s = _dot(q_ref[p][sub * ATT_TQ:(sub + 1) * ATT_TQ, lanes], jnp.concatenate(kts, axis=1))
            s_scr[h] = jnp.concatenate([s[:, :n_loc] + comb_scr[variant, h], s[:, n_loc:]], axis=1)
        for p in range(HEAD_PAIRS):
            outs = []
            for hh in range(2):
                lanes = slice(hh * HEAD_DIM, (hh + 1) * HEAD_DIM)
                vals = ([vp_ref[p][:, lanes]] if sub == 0 else []) + [vm_ref[p][mids[sub], lanes]]
                vals += ([vn_ref[p][:, lanes]] if sub == ATT_SUB - 1 else []) + [vx_ref[p][:, lanes]]
                outs.append(_softmax_pv(s_scr[2 * p + hh], jnp.concatenate(vals, axis=0)))
            o_ref[p, sub * ATT_TQ:(sub + 1) * ATT_TQ, :] = jnp.concatenate(outs, axis=1).astype(o_ref.dtype)


def _neighbourhood_attention(q, kt, v, ktx, vx, bias, masks, layer, batch, ctx_len):
    assert ctx_len == ATT_TQ
    n = q.shape[1]
    nblk = n // batch // ATT_TQ
    nstep = nblk // ATT_SUB
    wide = lambda b, i: b * nstep + i
    prev = lambda b, i: b * nblk + jnp.maximum(ATT_SUB * i - 1, 0)
    nxt = lambda b, i: b * nblk + jnp.minimum(ATT_SUB * (i + 1), nblk - 1)
    rows = lambda f, w: pl.BlockSpec((HEAD_PAIRS, w * ATT_TQ, V7X_LANES), lambda b, i: (0, f(b, i), 0))
    cols = lambda f, w: pl.BlockSpec((NA_WIDTH, w * ATT_TQ), lambda b, i: (0, f(b, i)))
    comb = (masks.shape[0],) + bias.shape[1:]
    vmem = ((int(np.prod(comb)) + bias[0].size + masks.size) * 4 + 16 * (ATT_SUB + 3) * ATT_TQ * NA_WIDTH * 2
            + (NA_HEADS * 4 + 16) * ATT_TQ * ATT_TQ * 4)
    return pl.pallas_call(
        _na_kernel,
        out_shape=jax.ShapeDtypeStruct((HEAD_PAIRS, n, V7X_LANES), BF16),
        grid=(batch, nstep),
        in_specs=[rows(wide, ATT_SUB), cols(prev, 1), cols(wide, ATT_SUB), cols(nxt, 1),
                  rows(prev, 1), rows(wide, ATT_SUB), rows(nxt, 1),
                  cols(lambda b, i: b, 1), rows(lambda b, i: b, 1),
                  _resident(bias.shape[1:], (layer,)), _resident(masks.shape)],
        out_specs=rows(wide, ATT_SUB),
        scratch_shapes=[pltpu.VMEM(comb, F32), pltpu.VMEM((NA_HEADS, ATT_TQ, 4 * ATT_TQ), F32)],
        compiler_params=pltpu.CompilerParams(
            dimension_semantics=("arbitrary", "arbitrary"), vmem_limit_bytes=_vmem_limit(vmem)),
        name="neighbourhood_attention",
    )(q, kt, kt, kt, v, v, v, ktx, vx, bias, masks)


def _ctx_attn_kernel(q_ref, kt_ref, v_ref, o_ref):
    for p in range(HEAD_PAIRS):
        outs = []
        for hh in range(2):
            lanes = slice(hh * HEAD_DIM, (hh + 1) * HEAD_DIM)
            s = _dot(q_ref[p][:, lanes], kt_ref[pl.ds(p * V7X_LANES + hh * HEAD_DIM, HEAD_DIM), :])
            outs.append(_softmax_pv(s, v_ref[p][:, lanes]))
        o_ref[p] = jnp.concatenate(outs, axis=1).astype(o_ref.dtype)


def _context_attention(q, kt, v, batch):
    n = q.shape[1]
    spec = pl.BlockSpec((HEAD_PAIRS, n // batch, V7X_LANES), lambda b: (0, b, 0))
    return pl.pallas_call(
        _ctx_attn_kernel,
        out_shape=jax.ShapeDtypeStruct((HEAD_PAIRS, n, V7X_LANES), BF16),
        grid=(batch,),
        in_specs=[spec, pl.BlockSpec((NA_WIDTH, n // batch), lambda b: (0, b)), spec],
        out_specs=spec,
        compiler_params=pltpu.CompilerParams(dimension_semantics=("arbitrary",)),
        name="context_attention",
    )(q, kt, v)


def _rope_tables(seq):
    t = np.arange(seq)
    inv = ROPE_THETA ** (-np.arange(ROPE_PAIRS, dtype=np.float64) / ROPE_PAIRS)
    ang_r = (t // GRID_W)[:, None] * inv
    ang_c = (t % GRID_W)[:, None] * inv
    ang = np.concatenate([ang_r, ang_r, ang_c, ang_c], axis=1)
    first = np.tile(np.arange(HEAD_DIM) % (2 * ROPE_PAIRS) < ROPE_PAIRS, 2)
    cos = np.tile(np.cos(ang), (1, 2))
    sin = np.tile(np.sin(ang), (1, 2))
    sa = np.where(first, -sin, 0.0)
    sb = np.where(first, 0.0, sin)
    return tuple(jnp.asarray(a, F32) for a in (cos, sa, sb))


def _identity_rope_tables(n):
    ones = jnp.ones((n, V7X_LANES), F32)
    zeros = jnp.zeros((n, V7X_LANES), F32)
    return ones, zeros, zeros


def _pool_tables(tm, tb, length):
    assert tb % length == 0 and tm % tb == 0
    t = np.arange(tm)
    pos = t % length
    base = t - pos
    bands, icnts = [], []
    for w in POOL_WINDOWS:
        lo = np.clip(pos - w // 2, 0, length)
        hi = np.clip(pos - w // 2 + w, 0, length)
        s = t[None, :tb]
        bands.append((s >= (base + lo)[:tb, None]) & (s < (base + hi)[:tb, None]))
        icnts.append(np.repeat((1.0 / (hi - lo))[:, None], POOL_CH, axis=1))
    band = jnp.asarray(np.stack(bands), BF16)
    icnt = jnp.asarray(np.concatenate(icnts, axis=1), F32)
    return band, icnt


def _attention_bias(rpb, rows):
    kh = min(NA_KH, rows)
    assert kh == NA_KH and ATT_ROWS == NA_KH // 2 and rows >= 3 * ATT_ROWS
    rpb = rpb * LOG2_E
    by_row = jnp.stack([rpb[:, :, ATT_ROWS - 1 - i:ATT_ROWS - 1 - i + 3 * ATT_ROWS] for i in range(ATT_ROWS)], axis=2)
    pad = GRID_W - NA_KW
    padded = jnp.pad(by_row, ((0, 0),) * 4 + ((pad, pad),))
    toep = jnp.stack([padded[..., GRID_W - 1 - j:2 * GRID_W - 1 - j] for j in range(GRID_W)], axis=4)
    base = jnp.transpose(toep, (0, 1, 2, 4, 3, 5)).reshape(rpb.shape[:2] + (ATT_TQ, 3 * ATT_TQ))

    i = np.arange(ATT_TQ) // GRID_W
    j = np.arange(ATT_TQ) % GRID_W
    m = np.arange(3 * ATT_TQ) // GRID_W
    c = np.arange(3 * ATT_TQ) % GRID_W
    col_start = np.clip(j - NA_KW // 2, 0, GRID_W - NA_KW)
    col_ok = (c[None, :] >= col_start[:, None]) & (c[None, :] < col_start[:, None] + NA_KW)
    masks = []
    for r0 in (0, ATT_ROWS, rows - ATT_ROWS):
        r = r0 + i
        kr = r0 - ATT_ROWS + m
        row_start = np.clip(r - kh // 2, 0, rows - kh)
        row_ok = (kr[None, :] >= row_start[:, None]) & (kr[None, :] < row_start[:, None] + kh)
        masks.append(np.where(row_ok & col_ok, 0.0, NEG_INF))
    return base.astype(F32), jnp.asarray(np.stack(masks), F32)


def kernel(x, c, ctx, c_ctx, w_mod, b_mod, norm_g, w_ffn_gate_up, w_ffn_down, w_in, w_out, na_rpb, w_pool, pool_scale):
    batch, seq, d = x.shape
    ctx_len = ctx.shape[1]
    depth = w_mod.shape[0]
    rows = seq // GRID_W
    tm_x, tm_c = 512, batch * ctx_len

    cv = jnp.concatenate([c, c_ctx[None], jnp.zeros((V7X_SUBLANES - batch - 1, d), F32)])
    mods = _mod_vectors(cv, w_mod, b_mod).reshape(depth, V7X_SUBLANES, N_MOD, 1, d)

    wgu, wdn = w_ffn_gate_up.astype(BF16), w_ffn_down.astype(BF16)
    w_in_b, w_out_b, w_pool_b = w_in.astype(BF16), w_out.astype(BF16), w_pool.astype(BF16)
    norm = norm_g.reshape(depth, norm_g.shape[1], 1, d)
    ps = pool_scale.reshape(depth, 1, POOL_WIDTH)
    bias, masks = _attention_bias(na_rpb, rows)

    rope_x = _rope_tables(seq)
    rope_c = _identity_rope_tables(tm_c)
    band_x, icnt_x = _pool_tables(tm_x, 2 * GRID_W, GRID_W)
    band_c, icnt_c = _pool_tables(tm_c, ctx_len, ctx_len)

    xt = x.reshape(batch * seq, d)
    ct = ctx.reshape(batch * ctx_len, d)
    for l in range(depth):
        last = l == depth - 1
        cx = _Cond(mods, norm, l, 0, batch)
        cc = _Cond(mods, norm, l, batch, 1)
        xt, qx, ktx, vx, px = _ffn(xt, cx, wgu, wdn, 0, tm_x, proj=(w_in_b, rope_x, band_x, icnt_x, w_pool_b, ps))
        ct, qc, ktc, vc, pc = _ffn(ct, cc, wgu, wdn, 0, tm_c, proj=(w_in_b, rope_c, band_c, icnt_c, w_pool_b, ps))
        ax = _neighbourhood_attention(qx, ktx, vx, ktc, vc, bias, masks, l, batch, ctx_len)
        xt = _ffn(xt, cx, wgu, wdn, 1, tm_x, mix=(ax, px, w_out_b))
        if not last:
            ac = _context_attention(qc, ktc, vc, batch)
            ct = _ffn(ct, cc, wgu, wdn, 1, tm_c, mix=(ac, pc, w_out_b))
    return xt.reshape(batch, seq, d)
```

```python
import functools

import numpy as np
import jax
import jax.numpy as jnp
from jax import lax
from jax.experimental import pallas as pl
from jax.experimental.pallas import tpu as pltpu

GRID_W = 64
N_MOD = 9
NA_HEADS = 8
HEAD_DIM = 64
NA_WIDTH = NA_HEADS * HEAD_DIM
NA_KH = 8
NA_KW = 16
POOL_GROUPS = 4
POOL_CH = 128
POOL_WIDTH = POOL_GROUPS * POOL_CH
POOL_WINDOWS = (2, 4, 8, 16)
ROPE_THETA = 10000.0
ROPE_PAIRS = HEAD_DIM // 4
RMS_EPS = 1e-6
NEG_INF = -1e30
LOG2_E = 1.4426950408889634

V7X_LANES = 128
V7X_SUBLANES = 8
V7X_VMEM_BYTES = 64 * 1024 * 1024
V7X_MXU_DIM = 256
COMPILER_SCRATCH_BYTES = 8 << 20
VMEM_RESERVE_BYTES = 6 << 20

HEAD_PAIRS = NA_WIDTH // V7X_LANES
FF_CHUNK = V7X_MXU_DIM
ATT_ROWS = NA_KH // 2
ATT_TQ = ATT_ROWS * GRID_W

F32 = jnp.float32
BF16 = jnp.bfloat16


def _vmem_limit(estimate_bytes):
    return int(min(estimate_bytes + COMPILER_SCRATCH_BYTES, V7X_VMEM_BYTES - VMEM_RESERVE_BYTES))


def _resident(block_shape, lead=()):
    index = tuple(lead) + (0,) * len(block_shape)
    return pl.BlockSpec((None,) * len(lead) + tuple(block_shape), lambda *_: index, pipeline_mode=pl.Buffered(1))


def _rms(x, g):
    return x * lax.rsqrt(jnp.mean(x * x, axis=-1, keepdims=True) + RMS_EPS) * g


def _dot(a, b):
    return jnp.dot(a, b, preferred_element_type=F32)


def _split_bf16(x):
    hi = x.astype(BF16)
    lo = (x - hi.astype(F32)).astype(BF16)
    return hi, lo


def _mod_kernel(cv_ref, w_ref, b_ref, o_ref):
    s = cv_ref[...]
    s = s * jax.nn.sigmoid(s)
    s_hi, s_lo = _split_bf16(s)
    w_hi, w_lo = _split_bf16(w_ref[...])
    n = s.shape[0]
    r = _dot(jnp.concatenate([s_hi, s_lo], axis=0), w_hi)
    o_ref[...] = r[:n] + r[n:] + _dot(s_hi, w_lo) + b_ref[...]


def _mod_vectors(cv, w_mod, b_mod):
    depth, d, n = w_mod.shape
    tn = 1536
    return pl.pallas_call(
        _mod_kernel,
        out_shape=jax.ShapeDtypeStruct((depth, cv.shape[0], n), F32),
        grid=(depth, n // tn),
        in_specs=[
            pl.BlockSpec(cv.shape, lambda l, j: (0, 0)),
            pl.BlockSpec((None, d, tn), lambda l, j: (l, 0, j)),
            pl.BlockSpec((None, 1, tn), lambda l, j: (l, 0, j)),
        ],
        out_specs=pl.BlockSpec((None, cv.shape[0], tn), lambda l, j: (l, 0, j)),
        compiler_params=pltpu.CompilerParams(
            dimension_semantics=("arbitrary", "arbitrary"),
            vmem_limit_bytes=_vmem_limit(5 * d * tn * 4)),
        name="mod_vectors",
    )(cv, w_mod, b_mod.reshape(depth, 1, n))


class _Cond:
    def __init__(self, mods, norm, layer, row0, n_rows):
        self.mods, self.norm, self.layer, self.row0, self.n_rows = mods, norm, layer, row0, n_rows

    def specs(self, n_tiles):
        per_row = n_tiles // self.n_rows
        layer, row0 = self.layer, self.row0
        return [pl.BlockSpec((None, None) + self.mods.shape[2:], lambda i: (layer, row0 + i // per_row, 0, 0, 0)),
                pl.BlockSpec((None,) + self.norm.shape[1:], lambda i: (layer, 0, 0, 0))]


def _ffn_kernel(*refs, mixed, projected, m0, g0):
    refs = list(refs)
    x_ref = refs.pop(0)
    mix_refs = [refs.pop(0) for _ in range(3)] if mixed else None
    mod_all, g_all, wgu_ref, wd_ref = (refs.pop(0) for _ in range(4))
    proj_in = [refs.pop(0) for _ in range(8)] if projected else None
    o_ref = refs.pop(0)
    proj_out = [refs.pop(0) for _ in range(4)] if projected else None
    h_scr, acc_scr = refs
    if mixed:
        a_ref, p_ref, wo_ref = mix_refs
        att = jnp.concatenate([a_ref[c] for c in range(a_ref.shape[0])], axis=1)
        y = _dot(att, wo_ref[pl.ds(0, NA_WIDTH), :]) + _dot(p_ref[...], wo_ref[pl.ds(NA_WIDTH, POOL_WIDTH), :])
        o_ref[...] = x_ref[...] + _rms(y, mod_all[m0] * g_all[g0])
        x_ref, m0, g0 = o_ref, m0 + 1, g0 + 1
    mod_ref, g_ref = mod_all.at[pl.ds(m0, 3)], g_all.at[pl.ds(g0, 2)]
    h = _rms(x_ref[...], g_ref[0] * (1.0 + mod_ref[1])) + mod_ref[0]
    h_scr[...] = h.astype(BF16)
    d_ff = wd_ref.shape[0]
    for j in range(d_ff // FF_CHUNK):
        hb = h_scr[...]
        gate = _dot(hb, wgu_ref[:, pl.ds(j * FF_CHUNK, FF_CHUNK)].astype(BF16))
        up = _dot(hb, wgu_ref[:, pl.ds(d_ff + j * FF_CHUNK, FF_CHUNK)].astype(BF16))
        act = (gate * jax.nn.sigmoid(gate) * up).astype(BF16)
        part = _dot(act, wd_ref[pl.ds(j * FF_CHUNK, FF_CHUNK), :].astype(BF16))
        if j == 0:
            acc_scr[...] = part
        else:
            acc_scr[...] += part
    o_ref[...] = x_ref[...] + _rms(acc_scr[...], 0.5 * mod_ref[2] * g_ref[1])
    if projected:
        _token_mix_inputs(o_ref, mod_all, g_all, *proj_in, *proj_out)


def _ffn(xt, cond, wgu, wd, half, tm, mix=None, proj=None):
    n, d = xt.shape
    nb = n // tm
    layer = (cond.layer,)
    which = (cond.layer, half)
    d_ff = wd.shape[2]
    assert d_ff % FF_CHUNK == 0 and wgu.shape[2:] == (d, 2 * d_ff)
    tok = pl.BlockSpec((tm, d), lambda i: (i, 0))
    vmem = 4 * tm * d * 4 + tm * d * 2 + tm * d * 4 + 3 * d * d_ff * wgu.dtype.itemsize + 4 * tm * FF_CHUNK * 4
    operands, specs = [xt], [tok]
    if mix is not None:
        att, pool, w_out = mix
        operands += [att, pool, w_out]
        specs += [pl.BlockSpec((att.shape[0], tm, att.shape[2]), lambda i: (0, i, 0)),
                  pl.BlockSpec((tm, pool.shape[1]), lambda i: (i, 0)), _resident(w_out.shape[1:], (cond.layer,))]
        vmem += 4 * tm * pool.shape[1] * 2 + 2 * d * d * 2 + 2 * tm * d * 4
    operands += [cond.mods, cond.norm, wgu, wd]
    specs += cond.specs(nb) + [_resident(wgu.shape[2:], which), _resident(wd.shape[2:], which)]
    out_shape, out_specs = [jax.ShapeDtypeStruct((n, d), F32)], [tok]
    if proj is not None:
        w_in, rope_tabs, band, icnt, w_pool, pool_scale = proj
        tab_blocks = rope_tabs[0].shape[0] // tm
        tab_spec = pl.BlockSpec((tm, V7X_LANES), lambda i: (i % tab_blocks, 0))
        operands += [w_in, *rope_tabs, band, icnt, w_pool, pool_scale]
        specs += [_resident(w_in.shape[1:], layer), tab_spec, tab_spec, tab_spec, _resident(band.shape),
                  _resident(icnt.shape), _resident(w_pool.shape[1:], layer), _resident(pool_scale.shape[1:], layer)]
        pair = jax.ShapeDtypeStruct((HEAD_PAIRS, n, V7X_LANES), BF16)
        pair_spec = pl.BlockSpec((HEAD_PAIRS, tm, V7X_LANES), lambda i: (0, i, 0))
        out_shape += [pair, jax.ShapeDtypeStruct((NA_WIDTH, n), BF16), pair, jax.ShapeDtypeStruct((n, POOL_WIDTH), BF16)]
        out_specs += [pair_spec, pl.BlockSpec((NA_WIDTH, tm), lambda i: (0, i)), pair_spec,
                      pl.BlockSpec((tm, POOL_WIDTH), lambda i: (i, 0))]
        vmem += (w_in[0].size * 2 + 6 * tm * V7X_LANES * 4 + band.size * 2 + icnt.size * 4
                 + 8 * tm * NA_WIDTH * 2 + 6 * tm * NA_WIDTH * 4)
    m0, g0 = (5, 3) if mix is not None else (3 * 2 * half, 4 * half)
    outs = pl.pallas_call(
        functools.partial(_ffn_kernel, mixed=mix is not None, projected=proj is not None, m0=m0, g0=g0),
        out_shape=out_shape,
        grid=(nb,),
        in_specs=specs,
        out_specs=out_specs,
        scratch_shapes=[pltpu.VMEM((tm, d), BF16), pltpu.VMEM((tm, d), F32)],
        compiler_params=pltpu.CompilerParams(
            dimension_semantics=("arbitrary",), vmem_limit_bytes=_vmem_limit(vmem)),
        name="mix_ffn" if mix is not None else ("ffn_proj" if proj is not None else "sandwich_ffn"),
    )(*operands)
    return outs if proj is not None else outs[0]


def _token_mix_inputs(x_ref, mod_ref, g_ref, w_ref, cos_ref, sa_ref, sb_ref, band_ref, icnt_ref, wp_ref, ps_ref,
                      q_ref, kt_ref, v_ref, p_ref):
    h = (_rms(x_ref[...], g_ref[2] * (1.0 + mod_ref[4])) + mod_ref[3]).astype(BF16)
    cos, sa, sb = cos_ref[...], sa_ref[...], sb_ref[...]

    def rope(t):
        up = pltpu.roll(t, V7X_LANES - ROPE_PAIRS, 1)
        down = pltpu.roll(t, ROPE_PAIRS, 1)
        return t * cos + up * sa + down * sb

    scale = HEAD_DIM ** -0.5 * LOG2_E
    qk =_dot(h, w_ref[:, pl.ds(0, 2 * NA_WIDTH)])
    for c in range(NA_WIDTH // V7X_LANES):
        lanes = pl.ds(c * V7X_LANES, V7X_LANES)
        q_ref[c] = (rope(qk[:, c * V7X_LANES:(c + 1) * V7X_LANES]) * scale).astype(BF16)
        k_rot = rope(qk[:, NA_WIDTH + c * V7X_LANES:NA_WIDTH + (c + 1) * V7X_LANES])
        kt_ref[lanes, :] = k_rot.T.astype(BF16)
    vu = _dot(h, w_ref[:, pl.ds(2 * NA_WIDTH, NA_WIDTH + POOL_WIDTH)])
    for c in range(NA_WIDTH // V7X_LANES):
        v_ref[c] = vu[:, c * V7X_LANES:(c + 1) * V7X_LANES].astype(BF16)
    tm, tb = x_ref.shape[0], band_ref.shape[1]
    for g in range(POOL_GROUPS):
        lanes = pl.ds(g * POOL_CH, POOL_CH)
        u = vu[:, NA_WIDTH + g * POOL_CH:NA_WIDTH + (g + 1) * POOL_CH]
        u2 = jnp.concatenate(_split_bf16(u), axis=1)
        band = band_ref[g]
        win2 = jnp.concatenate([_dot(band, u2[s * tb:(s + 1) * tb]) for s in range(tm // tb)], axis=0)
        win = win2[:, :POOL_CH] + win2[:, POOL_CH:]
        dlt = (win * icnt_ref[:, lanes] - u).astype(BF16)
        p_ref[:, lanes] = (_dot(dlt, wp_ref[g]) * ps_ref[:, lanes]).astype(BF16)


def _softmax_pv(s, v):
    e = jnp.exp2(s - jnp.max(s, axis=-1, keepdims=True))
    return _dot(e.astype(BF16), v) / jnp.sum(e, axis=-1, keepdims=True)


ATT_SUB = 2


def _na_kernel(q_ref, kp_ref, km_ref, kn_ref, vp_ref, vm_ref, vn_ref, kx_ref, vx_ref, bias_ref, mask_ref, o_ref,
               comb_scr, s_scr):
    i, last = pl.program_id(1), pl.num_programs(1) - 1

    @pl.when((pl.program_id(0) == 0) & (i == 0))
    def _():
        for var in range(comb_scr.shape[0]):
            for h in range(NA_HEADS):
                comb_scr[var, h] = bias_ref[h] + mask_ref[var]

    n_loc = 3 * ATT_TQ
    mids = [slice(max(sub - 1, 0) * ATT_TQ, min(sub + 2, ATT_SUB) * ATT_TQ) for sub in range(ATT_SUB)]
    for sub in range(ATT_SUB):
        variant = 1
        if sub == 0:
            variant = jnp.where(i == 0, 0, variant)
        if sub == ATT_SUB - 1:
            variant = jnp.where(i == last, 2, variant)
        for h in range(NA_HEADS):
            p, lanes = h // 2, slice((h % 2) * HEAD_DIM, (h % 2 + 1) * HEAD_DIM)
            feat = pl.ds(h * HEAD_DIM, HEAD_DIM)
            kts = ([kp_ref[feat, :]] if sub == 0 else []) + [km_ref[feat, mids[sub]]]
            kts += ([kn_ref[feat, :]] if sub == ATT_SUB - 1 else []) + [kx_ref[feat, :]]
            s = _dot(q_ref[p][sub * ATT_TQ:(sub + 1) * ATT_TQ, lanes], jnp.concatenate(kts, axis=1))
            s_scr[h] = jnp.concatenate([s[:, :n_loc] + comb_scr[variant, h], s[:, n_loc:]], axis=1)
        for p in range(HEAD_PAIRS):
            outs = []
            for hh in range(2):
                lanes = slice(hh * HEAD_DIM, (hh + 1) * HEAD_DIM)
                vals = ([vp_ref[p][:, lanes]] if sub == 0 else []) + [vm_ref[p][mids[sub], lanes]]
                vals += ([vn_ref[p][:, lanes]] if sub == ATT_SUB - 1 else []) + [vx_ref[p][:, lanes]]
                outs.append(_softmax_pv(s_scr[2 * p + hh], jnp.concatenate(vals, axis=0)))
            o_ref[p, sub * ATT_TQ:(sub + 1) * ATT_TQ, :] = jnp.concatenate(outs, axis=1).astype(o_ref.dtype)


def _neighbourhood_attention(q, kt, v, ktx, vx, bias, masks, layer, batch, ctx_len):
    assert ctx_len == ATT_TQ
    n = q.shape[1]
    nblk = n // batch // ATT_TQ
    nstep = nblk // ATT_SUB
    wide = lambda b, i: b * nstep + i
    prev = lambda b, i: b * nblk + jnp.maximum(ATT_SUB * i - 1, 0)
    nxt = lambda b, i: b * nblk + jnp.minimum(ATT_SUB * (i + 1), nblk - 1)
    rows = lambda f, w: pl.BlockSpec((HEAD_PAIRS, w * ATT_TQ, V7X_LANES), lambda b, i: (0, f(b, i), 0))
    cols = lambda f, w: pl.BlockSpec((NA_WIDTH, w * ATT_TQ), lambda b, i: (0, f(b, i)))
    comb = (masks.shape[0],) + bias.shape[1:]
    vmem = ((int(np.prod(comb)) + bias[0].size + masks.size) * 4 + 16 * (ATT_SUB + 3) * ATT_TQ * NA_WIDTH * 2
            + (NA_HEADS * 4 + 16) * ATT_TQ * ATT_TQ * 4)
    return pl.pallas_call(
        _na_kernel,
        out_shape=jax.ShapeDtypeStruct((HEAD_PAIRS, n, V7X_LANES), BF16),
        grid=(batch, nstep),
        in_specs=[rows(wide, ATT_SUB), cols(prev, 1), cols(wide, ATT_SUB), cols(nxt, 1),
                  rows(prev, 1), rows(wide, ATT_SUB), rows(nxt, 1),
                  cols(lambda b, i: b, 1), rows(lambda b, i: b, 1),
                  _resident(bias.shape[1:], (layer,)), _resident(masks.shape)],
        out_specs=rows(wide, ATT_SUB),
        scratch_shapes=[pltpu.VMEM(comb, F32), pltpu.VMEM((NA_HEADS, ATT_TQ, 4 * ATT_TQ), F32)],
        compiler_params=pltpu.CompilerParams(
            dimension_semantics=("arbitrary", "arbitrary"), vmem_limit_bytes=_vmem_limit(vmem)),
        name="neighbourhood_attention",
    )(q, kt, kt, kt, v, v, v, ktx, vx, bias, masks)


def _ctx_attn_kernel(q_ref, kt_ref, v_ref, o_ref):
    for p in range(HEAD_PAIRS):
        outs = []
        for hh in range(2):
            lanes = slice(hh * HEAD_DIM, (hh + 1) * HEAD_DIM)
            s = _dot(q_ref[p][:, lanes], kt_ref[pl.ds(p * V7X_LANES + hh * HEAD_DIM, HEAD_DIM), :])
            outs.append(_softmax_pv(s, v_ref[p][:, lanes]))
        o_ref[p] = jnp.concatenate(outs, axis=1).astype(o_ref.dtype)


def _context_attention(q, kt, v, batch):
    n = q.shape[1]
    spec = pl.BlockSpec((HEAD_PAIRS, n // batch, V7X_LANES), lambda b: (0, b, 0))
    return pl.pallas_call(
        _ctx_attn_kernel,
        out_shape=jax.ShapeDtypeStruct((HEAD_PAIRS, n, V7X_LANES), BF16),
        grid=(batch,),
        in_specs=[spec, pl.BlockSpec((NA_WIDTH, n // batch), lambda b: (0, b)), spec],
        out_specs=spec,
        compiler_params=pltpu.CompilerParams(dimension_semantics=("arbitrary",)),
        name="context_attention",
    )(q, kt, v)


def _rope_tables(seq):
    t = np.arange(seq)
    inv = ROPE_THETA ** (-np.arange(ROPE_PAIRS, dtype=np.float64) / ROPE_PAIRS)
    ang_r = (t // GRID_W)[:, None] * inv
    ang_c = (t % GRID_W)[:, None] * inv
    ang = np.concatenate([ang_r, ang_r, ang_c, ang_c], axis=1)
    first = np.tile(np.arange(HEAD_DIM) % (2 * ROPE_PAIRS) < ROPE_PAIRS, 2)
    cos = np.tile(np.cos(ang), (1, 2))
    sin = np.tile(np.sin(ang), (1, 2))
    sa = np.where(first, -sin, 0.0)
    sb = np.where(first, 0.0, sin)
    return tuple(jnp.asarray(a, F32) for a in (cos, sa, sb))


def _identity_rope_tables(n):
    ones = jnp.ones((n, V7X_LANES), F32)
    zeros = jnp.zeros((n, V7X_LANES), F32)
    return ones, zeros, zeros


def _pool_tables(tm, tb, length):
    assert tb % length == 0 and tm % tb == 0
    t = np.arange(tm)
    pos = t % length
    base = t - pos
    bands, icnts = [], []
    for w in POOL_WINDOWS:
        lo = np.clip(pos - w // 2, 0, length)
        hi = np.clip(pos - w // 2 + w, 0, length)
        s = t[None, :tb]
        bands.append((s >= (base + lo)[:tb, None]) & (s < (base + hi)[:tb, None]))
        icnts.append(np.repeat((1.0 / (hi - lo))[:, None], POOL_CH, axis=1))
    band = jnp.asarray(np.stack(bands), BF16)
    icnt = jnp.asarray(np.concatenate(icnts, axis=1), F32)
    return band, icnt


def _attention_bias(rpb, rows):
    kh = min(NA_KH, rows)
    assert kh == NA_KH and ATT_ROWS == NA_KH // 2 and rows >= 3 * ATT_ROWS
    rpb = rpb * LOG2_E
    by_row = jnp.stack([rpb[:, :, ATT_ROWS - 1 - i:ATT_ROWS - 1 - i + 3 * ATT_ROWS] for i in range(ATT_ROWS)], axis=2)
    pad = GRID_W - NA_KW
    padded = jnp.pad(by_row, ((0, 0),) * 4 + ((pad, pad),))
    toep = jnp.stack([padded[..., GRID_W - 1 - j:2 * GRID_W - 1 - j] for j in range(GRID_W)], axis=4)
    base = jnp.transpose(toep, (0, 1, 2, 4, 3, 5)).reshape(rpb.shape[:2] + (ATT_TQ, 3 * ATT_TQ))

    i = np.arange(ATT_TQ) // GRID_W
    j = np.arange(ATT_TQ) % GRID_W
    m = np.arange(3 * ATT_TQ) // GRID_W
    c = np.arange(3 * ATT_TQ) % GRID_W
    col_start = np.clip(j - NA_KW // 2, 0, GRID_W - NA_KW)
    col_ok = (c[None, :] >= col_start[:, None]) & (c[None, :] < col_start[:, None] + NA_KW)
    masks = []
    for r0 in (0, ATT_ROWS, rows - ATT_ROWS):
        r = r0 + i
        kr = r0 - ATT_ROWS + m
        row_start = np.clip(r - kh // 2, 0, rows - kh)
        row_ok = (kr[None, :] >= row_start[:, None]) & (kr[None, :] < row_start[:, None] + kh)
        masks.append(np.where(row_ok & col_ok, 0.0, NEG_INF))
    return base.astype(F32), jnp.asarray(np.stack(masks), F32)


def kernel(x, c, ctx, c_ctx, w_mod, b_mod, norm_g, w_ffn_gate_up, w_ffn_down, w_in, w_out, na_rpb, w_pool, pool_scale):
    batch, seq, d = x.shape
    ctx_len = ctx.shape[1]
    depth = w_mod.shape[0]
    rows = seq // GRID_W
    tm_x, tm_c = 512, batch * ctx_len

    cv = jnp.concatenate([c, c_ctx[None], jnp.zeros((V7X_SUBLANES - batch - 1, d), F32)])
    mods = _mod_vectors(cv, w_mod, b_mod).reshape(depth, V7X_SUBLANES, N_MOD, 1, d)

    wgu, wdn = w_ffn_gate_up, w_ffn_down
    w_in_b, w_out_b, w_pool_b = w_in.astype(BF16), w_out.astype(BF16), w_pool.astype(BF16)
    norm = norm_g.reshape(depth, norm_g.shape[1], 1, d)
    ps = pool_scale.reshape(depth, 1, POOL_WIDTH)
    bias, masks = _attention_bias(na_rpb, rows)

    rope_x = _rope_tables(seq)
    rope_c = _identity_rope_tables(tm_c)
    band_x, icnt_x = _pool_tables(tm_x, 2 * GRID_W, GRID_W)
    band_c, icnt_c = _pool_tables(tm_c, ctx_len, ctx_len)

    xt = x.reshape(batch * seq, d)
    ct = ctx.reshape(batch * ctx_len, d)
    for l in range(depth):
        last = l == depth - 1
        cx = _Cond(mods, norm, l, 0, batch)
        cc = _Cond(mods, norm, l, batch, 1)
        xt, qx, ktx, vx, px = _ffn(xt, cx, wgu, wdn, 0, tm_x, proj=(w_in_b, rope_x, band_x, icnt_x, w_pool_b, ps))
        ct, qc, ktc, vc, pc = _ffn(ct, cc, wgu, wdn, 0, tm_c, proj=(w_in_b, rope_c, band_c, icnt_c, w_pool_b, ps))
        ax = _neighbourhood_attention(qx, ktx, vx, ktc, vc, bias, masks, l, batch, ctx_len)
        xt = _ffn(xt, cx, wgu, wdn, 1, tm_x, mix=(ax, px, w_out_b))
        if not last:
            ac = _context_attention(qc, ktc, vc, batch)
            ct = _ffn(ct, cc, wgu, wdn, 1, tm_c, mix=(ac, pc, w_out_b))
    return xt.reshape(batch, seq, d)
```
